```python
import math
import jax, jax.numpy as jnp
from jax import lax
import numpy as np

D_MODEL = 1024
BATCH = 8
SEQ = 4096
DEPTH = 2

MLSTM_HEADS = 4
MLSTM_HEAD_DIM = D_MODEL // 8
MLSTM_WIDTH = MLSTM_HEADS * MLSTM_HEAD_DIM
MLSTM_CHUNK = 64
MLA_HEADS = 4
MLA_NOPE_DIM = D_MODEL // 8
MLA_ROPE_DIM = 64
MLA_V_DIM = D_MODEL // 8
MLA_Q_LORA = D_MODEL // 4
MLA_KV_LORA = D_MODEL // 8
MLA_WIDTH = MLA_HEADS * MLA_V_DIM
ROPE_THETA = 10000.0
ATTN_BLOCK = 128
POOL_WINDOWS = (2, 4, 8, 16)
POOL_GROUPS = len(POOL_WINDOWS)
POOL_GROUP_DIM = D_MODEL // POOL_GROUPS
FFN_DIM = ((8 * D_MODEL // 3 + 127) // 128) * 128
CONV_WIDTH = 3
LN_EPS = 1e-5
RMS_EPS = 1e-6
DEEPNORM_ALPHA = (2 * DEPTH) ** 0.25
DEEPNORM_BETA = (8 * DEPTH) ** -0.25
N_EVEN = (DEPTH + 1) // 2
N_ODD = DEPTH // 2
IN_SIZES = (MLSTM_WIDTH, MLSTM_WIDTH, MLSTM_WIDTH, MLSTM_WIDTH, MLSTM_HEADS, MLSTM_HEADS,
            MLA_Q_LORA, MLA_KV_LORA, MLA_ROPE_DIM)
IN_COLS = sum(IN_SIZES)

kernel_name = "hybrid_mlstm_mla_pool_deepnorm"


def layer_norm(x, g, b):
    xf = x.astype(jnp.float32)
    mu = jnp.mean(xf, axis=-1, keepdims=True)
    var = jnp.mean(jnp.square(xf - mu), axis=-1, keepdims=True)
    return ((xf - mu) * lax.rsqrt(var + LN_EPS) * g + b).astype(x.dtype)


def rms_norm(x, g):
    xf = x.astype(jnp.float32)
    return (xf * lax.rsqrt(jnp.mean(jnp.square(xf), axis=-1, keepdims=True) + RMS_EPS) * g).astype(x.dtype)


def rope_tables(positions):
    inv_freq = ROPE_THETA ** (-jnp.arange(0, MLA_ROPE_DIM, 2, dtype=jnp.float32) / MLA_ROPE_DIM)
    ang = positions.astype(jnp.float32)[..., None] * inv_freq
    return jnp.cos(ang), jnp.sin(ang)


def apply_rope(x, cos, sin):
    xf = x.astype(jnp.float32)
    x1, x2 = jnp.split(xf, 2, axis=-1)
    return jnp.concatenate([x1 * cos - x2 * sin, x2 * cos + x1 * sin], axis=-1).astype(x.dtype)


def mlstm_chunkwise(q, k, v, i_pre, f_pre):
    B, H, S, d = q.shape
    L = MLSTM_CHUNK
    nc = S // L
    f32 = jnp.float32

    def chunks(t):
        return jnp.moveaxis(t.astype(f32).reshape(B, H, nc, L, *t.shape[3:]), 2, 0)

    qc, kc, vc = chunks(q), chunks(k), chunks(v)
    ic = chunks(i_pre)
    bc = jnp.cumsum(chunks(jax.nn.log_sigmoid(f_pre.astype(f32))), axis=-1)
    causal = jnp.tril(jnp.ones((L, L), dtype=bool))

    def step(carry, xs):
        C, n, m = carry
        q_, k_, v_, i_, b_ = xs
        D = jnp.where(causal, b_[..., :, None] - b_[..., None, :] + i_[..., None, :], -jnp.inf)
        inter = b_ + m[..., None]
        m_t = jnp.maximum(jnp.max(D, axis=-1), inter)
        A = jnp.einsum('bhtk,bhsk->bhts', q_, k_) * jnp.exp(D - m_t[..., None])
        sc = jnp.exp(inter - m_t)
        num = jnp.einsum('bhts,bhsv->bhtv', A, v_) + sc[..., None] * jnp.einsum('bhvk,bhtk->bhtv', C, q_)
        den = jnp.sum(A, axis=-1) + sc * jnp.einsum('bhk,bhtk->bht', n, q_)
        h = num / jnp.maximum(jnp.abs(den), jnp.exp(-m_t))[..., None]
        b_last = b_[..., -1]
        g = b_last[..., None] - b_ + i_
        m_new = jnp.maximum(b_last + m, jnp.max(g, axis=-1))
        w = jnp.exp(g - m_new[..., None])
        decay = jnp.exp(b_last + m - m_new)
        C = decay[..., None, None] * C + jnp.einsum('bhs,bhsv,bhsk->bhvk', w, v_, k_)
        n = decay[..., None] * n + jnp.einsum('bhs,bhsk->bhk', w, k_)
        return (C, n, m_new), h

    init = (jnp.zeros((B, H, d, d), f32), jnp.zeros((B, H, d), f32), jnp.zeros((B, H), f32))
    _, h = lax.scan(step, init, (qc, kc, vc, ic, bc))
    return jnp.moveaxis(h, 0, 2).reshape(B, H, S, d).astype(q.dtype)


def causal_attention_blocked(q, k, v):
    B, H, S, dk = q.shape
    nb = S // ATTN_BLOCK
    scale = dk ** -0.5
    qb = jnp.moveaxis(q.reshape(B, H, nb, ATTN_BLOCK, dk), 2, 0)
    key_pos = jnp.arange(S)

    def one_block(args):
        idx, qblk = args
        s = jnp.einsum('bhqd,bhkd->bhqk', qblk, k, preferred_element_type=jnp.float32) * scale
        q_pos = idx * ATTN_BLOCK + jnp.arange(ATTN_BLOCK)
        mask = key_pos[None, :] <= q_pos[:, None]
        p = jax.nn.softmax(jnp.where(mask, s, -jnp.inf), axis=-1)
        return jnp.einsum('bhqk,bhkd->bhqd', p.astype(v.dtype), v)

    o = lax.map(one_block, (jnp.arange(nb), qb))
    return jnp.moveaxis(o, 0, 2).reshape(B, H, S, v.shape[-1])


def hybrid_mixer(x, cos, sin, w_in, b_igate, b_fgate, mlstm_norm, q_norm, kv_norm, w_uq, w_ukv, w_out):
    B, S, _ = x.shape
    h = x @ w_in
    offs = np.cumsum(IN_SIZES)[:-1].tolist()
    q_m, k_m, v_m, o_m, i_pre, f_pre, c_q, c_kv, k_r = jnp.split(h, offs, axis=-1)

    def heads(t, nh):
        return t.reshape(B, S, nh, -1).transpose(0, 2, 1, 3)

    hm = mlstm_chunkwise(heads(q_m, MLSTM_HEADS),
                         heads(k_m, MLSTM_HEADS) * (MLSTM_HEAD_DIM ** -0.5),
                         heads(v_m, MLSTM_HEADS),
                         (i_pre + b_igate).transpose(0, 2, 1),
                         (f_pre + b_fgate).transpose(0, 2, 1))
    hm = rms_norm(hm.transpose(0, 2, 1, 3), mlstm_norm.reshape(MLSTM_HEADS, MLSTM_HEAD_DIM))
    y_m = (hm * jax.nn.sigmoid(o_m.reshape(B, S, MLSTM_HEADS, MLSTM_HEAD_DIM))).reshape(B, S, MLSTM_WIDTH)

    q = (rms_norm(c_q, q_norm) @ w_uq).reshape(B, S, MLA_HEADS, MLA_NOPE_DIM + MLA_ROPE_DIM)
    q_nope, q_rope = q[..., :MLA_NOPE_DIM], q[..., MLA_NOPE_DIM:]
    q_rope = apply_rope(q_rope, cos[:, :, None, :], sin[:, :, None, :])
    kv = (rms_norm(c_kv, kv_norm) @ w_ukv).reshape(B, S, MLA_HEADS, MLA_NOPE_DIM + MLA_V_DIM)
    k_nope, v = kv[..., :MLA_NOPE_DIM], kv[..., MLA_NOPE_DIM:]
    k_rope = jnp.broadcast_to(apply_rope(k_r, cos, sin)[:, :, None, :], (B, S, MLA_HEADS, MLA_ROPE_DIM))
    qh = jnp.concatenate([q_nope, q_rope], axis=-1).transpose(0, 2, 1, 3)
    kh = jnp.concatenate([k_nope, k_rope], axis=-1).transpose(0, 2, 1, 3)
    y_a = causal_attention_blocked(qh, kh, v.transpose(0, 2, 1, 3))
    y_a = y_a.transpose(0, 2, 1, 3).reshape(B, S, MLA_WIDTH)

    return jnp.concatenate([y_m, y_a], axis=-1) @ w_out


def pool_mixer(x, pool_w, layer_scale):
    B, S, D = x.shape
    xf = x.astype(jnp.float32)
    cs = jnp.concatenate([jnp.zeros((B, 1, D), jnp.float32), jnp.cumsum(xf, axis=1)], axis=1)
    t = jnp.arange(S)
    outs = []
    for g, w in enumerate(POOL_WINDOWS):
        sl = slice(g * POOL_GROUP_DIM, (g + 1) * POOL_GROUP_DIM)
        start = jnp.maximum(t + 1 - w, 0)
        csg = cs[..., sl]
        mean = (csg[:, 1:] - csg[:, start]) / (t + 1 - start).astype(jnp.float32)[:, None]
        outs.append(mean - xf[..., sl])
    pooled = jnp.stack(outs, axis=2).astype(x.dtype)
    y = jnp.einsum('bsgc,gcd->bsgd', pooled, pool_w).reshape(B, S, D)
    return y * layer_scale


def conv_ffn(x, w_up, conv_w, conv_b, w_down):
    S = x.shape[1]
    u = x @ w_up
    up = jnp.pad(u, ((0, 0), (CONV_WIDTH - 1, 0), (0, 0)))
    u = sum(up[:, j:j + S] * conv_w[j] for j in range(CONV_WIDTH)) + conv_b
    gate, val = jnp.split(u, 2, axis=-1)
    return (jax.nn.silu(gate) * val) @ w_down


def setup_inputs(seed: int = 0) -> dict:
    key = jax.random.key(seed)
    ks = jax.random.split(key, 24)
    f32 = jnp.float32

    def nrm(k, shape, scale):
        return jax.random.normal(k, shape, f32) * scale

    x = jax.random.normal(ks[0], (BATCH, SEQ, D_MODEL), f32)
    positions = (jnp.arange(SEQ, dtype=jnp.int32)[None, :]
                 + jax.random.randint(ks[1], (BATCH, 1), 0, 1024, dtype=jnp.int32))
    v_lo = 2 * MLSTM_WIDTH
    even_w_in = nrm(ks[2], (N_EVEN, D_MODEL, IN_COLS), D_MODEL ** -0.5)
    even_w_in = even_w_in.at[..., v_lo:v_lo + MLSTM_WIDTH].multiply(DEEPNORM_BETA)
    even_b_igate = -2.0 + nrm(ks[3], (N_EVEN, MLSTM_HEADS), 0.1)
    even_b_fgate = jnp.linspace(3.0, 6.0, MLSTM_HEADS, dtype=f32)[None] + nrm(ks[4], (N_EVEN, MLSTM_HEADS), 0.1)
    even_mlstm_norm = 1.0 + nrm(ks[5], (N_EVEN, MLSTM_WIDTH), 0.05)
    even_q_norm = 1.0 + nrm(ks[6], (N_EVEN, MLA_Q_LORA), 0.05)
    even_kv_norm = 1.0 + nrm(ks[7], (N_EVEN, MLA_KV_LORA), 0.05)
    even_w_uq = nrm(ks[8], (N_EVEN, MLA_Q_LORA, MLA_HEADS * (MLA_NOPE_DIM + MLA_ROPE_DIM)), MLA_Q_LORA ** -0.5)
    w_ukv = nrm(ks[9], (N_EVEN, MLA_KV_LORA, MLA_HEADS, MLA_NOPE_DIM + MLA_V_DIM), MLA_KV_LORA ** -0.5)
    even_w_ukv = w_ukv.at[..., MLA_NOPE_DIM:].multiply(DEEPNORM_BETA).reshape(N_EVEN, MLA_KV_LORA, -1)
    even_w_out = nrm(ks[10], (N_EVEN, MLSTM_WIDTH + MLA_WIDTH, D_MODEL), DEEPNORM_BETA * (MLSTM_WIDTH + MLA_WIDTH) ** -0.5)
    odd_pool_w = nrm(ks[11], (N_ODD, POOL_GROUPS, POOL_GROUP_DIM, POOL_GROUP_DIM), DEEPNORM_BETA * POOL_GROUP_DIM ** -0.5)
    odd_layer_scale = 1.0 + nrm(ks[12], (N_ODD, D_MODEL), 0.1)
    ffn_w_up = nrm(ks[13], (DEPTH, D_MODEL, 2 * FFN_DIM), DEEPNORM_BETA * D_MODEL ** -0.5)
    ffn_conv_w = nrm(ks[14], (DEPTH, CONV_WIDTH, 2 * FFN_DIM), CONV_WIDTH ** -0.5)
    ffn_conv_b = nrm(ks[15], (DEPTH, 2 * FFN_DIM), 0.02)
    ffn_w_down = nrm(ks[16], (DEPTH, FFN_DIM, D_MODEL), DEEPNORM_BETA * FFN_DIM ** -0.5)
    ln_mix_g = 1.0 + nrm(ks[17], (DEPTH, D_MODEL), 0.05)
    ln_mix_b = nrm(ks[18], (DEPTH, D_MODEL), 0.02)
    ln_ffn_g = 1.0 + nrm(ks[19], (DEPTH, D_MODEL), 0.05)
    ln_ffn_b = nrm(ks[20], (DEPTH, D_MODEL), 0.02)
    return {"x": x, "positions": positions,
            "even_w_in": even_w_in, "even_b_igate": even_b_igate, "even_b_fgate": even_b_fgate,
            "even_mlstm_norm": even_mlstm_norm, "even_q_norm": even_q_norm, "even_kv_norm": even_kv_norm,
            "even_w_uq": even_w_uq, "even_w_ukv": even_w_ukv, "even_w_out": even_w_out,
            "odd_pool_w": odd_pool_w, "odd_layer_scale": odd_layer_scale,
            "ffn_w_up": ffn_w_up, "ffn_conv_w": ffn_conv_w, "ffn_conv_b": ffn_conv_b, "ffn_w_down": ffn_w_down,
            "ln_mix_g": ln_mix_g, "ln_mix_b": ln_mix_b, "ln_ffn_g": ln_ffn_g, "ln_ffn_b": ln_ffn_b}


def reference(x, positions, even_w_in, even_b_igate, even_b_fgate, even_mlstm_norm, even_q_norm,
              even_kv_norm, even_w_uq, even_w_ukv, even_w_out, odd_pool_w, odd_layer_scale,
              ffn_w_up, ffn_conv_w, ffn_conv_b, ffn_w_down, ln_mix_g, ln_mix_b, ln_ffn_g, ln_ffn_b):
    cos, sin = rope_tables(positions)
    for layer in range(DEPTH):
        e = layer // 2
        if layer % 2 == 0:
            y = hybrid_mixer(x, cos, sin, even_w_in[e], even_b_igate[e], even_b_fgate[e],
                             even_mlstm_norm[e], even_q_norm[e], even_kv_norm[e],
                             even_w_uq[e], even_w_ukv[e], even_w_out[e])
        else:
            y = pool_mixer(x, odd_pool_w[e], odd_layer_scale[e])
        x = layer_norm(DEEPNORM_ALPHA * x + y, ln_mix_g[layer], ln_mix_b[layer])
        y = conv_ffn(x, ffn_w_up[layer], ffn_conv_w[layer], ffn_conv_b[layer], ffn_w_down[layer])
        x = layer_norm(DEEPNORM_ALPHA * x + y, ln_ffn_g[layer], ln_ffn_b[layer])
    return x
```

```python
import functools

import jax
import jax.numpy as jnp
import numpy as np
from jax import lax
from jax.experimental import pallas as pl
from jax.experimental.pallas import tpu as pltpu

F32 = jnp.float32
BF16 = jnp.bfloat16

D_MODEL = 1024
DEPTH = 2
MLSTM_HEADS = 4
HEAD_DIM = 128
MLSTM_WIDTH = MLSTM_HEADS * HEAD_DIM
MLA_HEADS = 4
ROPE_DIM = 64
ROPE_HALF = ROPE_DIM // 2
Q_LORA = 256
KV_LORA = 128
MLA_WIDTH = MLA_HEADS * HEAD_DIM
ROPE_THETA = 10000.0
POOL_WINDOWS = (2, 4, 8, 16)
POOL_GROUP_DIM = D_MODEL // len(POOL_WINDOWS)
FFN_DIM = 2816
CONV_WIDTH = 3
LN_EPS = 1e-5
RMS_EPS = 1e-6
DEEPNORM_ALPHA = (2 * DEPTH) ** 0.25
IN_SIZES = (MLSTM_WIDTH, MLSTM_WIDTH, MLSTM_WIDTH, MLSTM_WIDTH, MLSTM_HEADS, MLSTM_HEADS,
            Q_LORA, KV_LORA, ROPE_DIM)

LANES = 128
SUBLANES = 8
MXU_DIM = 256
VMEM_LIMIT_BYTES = 56 * 1024 * 1024

ROPE_LO = 8
ROPE_HI = ROPE_LO + LANES // 2
QK_HEAD = 2 * LANES

ROW_TILE = 512
MLSTM_CHUNK = 128
MLSTM_BLOCK = 512
ATTN_TILE = 512
FFN_COLS = 256
POOL_HALO = 16


def _const_spec(shape):
    nd = len(shape)
    return pl.BlockSpec(shape, lambda *_: (0,) * nd, pipeline_mode=pl.Buffered(1))


def _layer_norm(z, g, b):
    mu = jnp.mean(z, axis=-1, keepdims=True)
    d = z - mu
    var = jnp.mean(d * d, axis=-1, keepdims=True)
    return d * lax.rsqrt(var + LN_EPS) * g + b


def _rms_norm(z, g):
    return z * lax.rsqrt(jnp.mean(z * z, axis=-1, keepdims=True) + RMS_EPS) * g


def _log_sigmoid(z):
    return -(jnp.maximum(-z, 0.0) + jnp.log1p(jnp.exp(-jnp.abs(z))))


def _split3(z):
    hi = z.astype(BF16)
    r1 = z - hi.astype(F32)
    mid = r1.astype(BF16)
    lo = (r1 - mid.astype(F32)).astype(BF16)
    return hi, mid, lo


def _inproj_kernel(x_ref, pos_ref, w_in_ref, qn_ref, kvn_ref, wuq_ref, wukv_ref, freq_ref, cmask_ref,
                   smask_ref, qm_ref, km_ref, vm_ref, om_ref, gates_ref, qa_ref, ka_ref, va_ref):
    xb = x_ref[...].astype(BF16)

    def proj(lo, width):
        return jnp.dot(xb, w_in_ref[:, lo:lo + width], preferred_element_type=F32)

    w = MLSTM_WIDTH
    qm_ref[...] = proj(0, w).astype(BF16)
    km_ref[...] = (proj(w, w) * (HEAD_DIM ** -0.5)).astype(BF16)
    vm_ref[...] = proj(2 * w, w).astype(BF16)
    om_ref[...] = proj(3 * w, w)

    ang = pos_ref[...] * freq_ref[...]
    cosm = jnp.cos(ang) * cmask_ref[...]
    sinm = jnp.sin(ang) * smask_ref[...]

    def rope(blk):
        return blk * cosm + pltpu.roll(blk, LANES // 2, 1) * sinm

    c_q = proj(4 * w, Q_LORA)
    c_kv = proj(4 * w + Q_LORA, KV_LORA)
    gk = proj(4 * w + Q_LORA + KV_LORA, LANES)
    gates_ref[...] = gk[:, 0:2 * MLSTM_HEADS]
    k_rope = rope(gk).astype(BF16)

    scale = (HEAD_DIM + ROPE_DIM) ** -0.5
    q = jnp.dot(_rms_norm(c_q, qn_ref[...]).astype(BF16), wuq_ref[...], preferred_element_type=F32)
    kv = jnp.dot(_rms_norm(c_kv, kvn_ref[...]).astype(BF16), wukv_ref[...], preferred_element_type=F32)
    for h in range(MLA_HEADS):
        lo = h * QK_HEAD
        qa_ref[:, lo:lo + LANES] = (q[:, lo:lo + LANES] * scale).astype(BF16)
        qa_ref[:, lo + LANES:lo + QK_HEAD] = (rope(q[:, lo + LANES:lo + QK_HEAD]) * scale).astype(BF16)
        ka_ref[:, lo:lo + LANES] = kv[:, h * HEAD_DIM:(h + 1) * HEAD_DIM].astype(BF16)
        ka_ref[:, lo + LANES:lo + QK_HEAD] = k_rope
    va_ref[...] = kv[:, MLA_WIDTH:].astype(BF16)


def _inproj(x2, pos2, w_in_p, q_norm, kv_norm, wuq_p, wukv_p, freq, cmask, smask):
    t = x2.shape[0]
    tm = ROW_TILE
    row = lambda width: pl.BlockSpec((tm, width), lambda i: (i, 0))
    out_shapes = (
        jax.ShapeDtypeStruct((t, MLSTM_WIDTH), BF16),
        jax.ShapeDtypeStruct((t, MLSTM_WIDTH), BF16),
        jax.ShapeDtypeStruct((t, MLSTM_WIDTH), BF16),
        jax.ShapeDtypeStruct((t, MLSTM_WIDTH), F32),
        jax.ShapeDtypeStruct((t, 2 * MLSTM_HEADS), F32),
        jax.ShapeDtypeStruct((t, MLA_HEADS * QK_HEAD), BF16),
        jax.ShapeDtypeStruct((t, MLA_HEADS * QK_HEAD), BF16),
        jax.ShapeDtypeStruct((t, MLA_WIDTH), BF16),
    )
    return pl.pallas_call(
        _inproj_kernel,
        out_shape=out_shapes,
        grid=(t // tm,),
        in_specs=[row(D_MODEL), row(1), _const_spec(w_in_p.shape), _const_spec(q_norm.shape),
                  _const_spec(kv_norm.shape), _const_spec(wuq_p.shape), _const_spec(wukv_p.shape),
                  _const_spec(freq.shape), _const_spec(cmask.shape), _const_spec(smask.shape)],
        out_specs=tuple(row(s.shape[1]) for s in out_shapes),
        compiler_params=pltpu.CompilerParams(dimension_semantics=("parallel",),
                                             vmem_limit_bytes=VMEM_LIMIT_BYTES),
        name="inproj",
    )(x2, pos2, w_in_p, q_norm, kv_norm, wuq_p, wukv_p, freq, cmask, smask)


def _mlstm_kernel(q_ref, k_ref, v_ref, o_ref, g_ref, gt_ref, bcol_ref, brow_ref, nw_ref,
                  y_ref, ct_ref, m_ref):
    L = MLSTM_CHUNK
    d = HEAD_DIM

    @pl.when(pl.program_id(1) == 0)
    def _():
        ct_ref[...] = jnp.zeros_like(ct_ref)
        m_ref[...] = jnp.zeros_like(m_ref)

    r_idx = lax.broadcasted_iota(jnp.int32, (L, L), 0)
    c_idx = lax.broadcasted_iota(jnp.int32, (L, L), 1)
    causal = r_idx >= c_idx
    tri = causal.astype(BF16)
    tri_t = (r_idx <= c_idx).astype(BF16)
    ones_blk = jnp.ones((L, d), BF16)
    neg_inf = jnp.float32(-jnp.inf)

    def chunk(c, carry):
        r0 = pl.multiple_of(c * L, L)
        g_col = g_ref[0, pl.ds(r0, L), :] + bcol_ref[...]
        g_row = gt_ref[0, c] + brow_ref[...]
        ls_col = _log_sigmoid(g_col)
        ls_row = _log_sigmoid(g_row)
        b_col = sum(jnp.dot(tri, p, preferred_element_type=F32) for p in _split3(ls_col))
        b_row = sum(jnp.dot(p, tri_t, preferred_element_type=F32) for p in _split3(ls_row))

        for h in range(MLSTM_HEADS):
            hs = slice(h * d, (h + 1) * d)
            fh = MLSTM_HEADS + h
            q = q_ref[0, pl.ds(r0, L), hs]
            k = k_ref[0, pl.ds(r0, L), hs]
            v = v_ref[0, pl.ds(r0, L), hs]
            a_col = g_col[:, h:h + 1] - b_col[:, fh:fh + 1]
            a_row = g_row[h:h + 1, :] - b_row[fh:fh + 1, :]
            bt_col = b_col[:, fh:fh + 1]
            b_last = b_row[fh:fh + 1, L - 1:L]
            m_prev = m_ref[h][0:1, 0:1]

            m_col = jnp.maximum(
                jnp.max(jnp.where(causal, a_row, neg_inf), axis=-1, keepdims=True), m_prev)
            wgt = jnp.exp(jnp.where(causal, a_row - m_col, neg_inf))
            s = lax.dot_general(q, k, (((1,), (1,)), ((), ())), preferred_element_type=F32)
            a_mat = (s * wgt).astype(BF16)
            v_aug = jnp.concatenate([v, ones_blk], axis=1)
            ct = ct_ref[h]
            tot = (jnp.dot(a_mat, v_aug, preferred_element_type=F32)
                   + jnp.exp(m_prev - m_col) * jnp.dot(q, ct.astype(BF16), preferred_element_type=F32))
            num = tot[:, :d]
            den = tot[:, d:]
            hraw = num / jnp.maximum(jnp.abs(den), jnp.exp(-(bt_col + m_col)))
            hn = _rms_norm(hraw, nw_ref[:, hs])
            gate = jax.nn.sigmoid(o_ref[0, pl.ds(r0, L), hs])
            y_ref[0, pl.ds(r0, L), hs] = (hn * gate).astype(BF16)

            m_last = m_col[L - 1:L, :]
            w_col = jnp.exp(a_col - m_last)
            wv = (v_aug.astype(F32) * w_col).astype(BF16)
            upd = lax.dot_general(k, wv, (((0,), (0,)), ((), ())), preferred_element_type=F32)
            ct_ref[h] = jnp.exp(m_prev - m_last) * ct + upd
            m_ref[h] = jnp.broadcast_to(b_last + m_last, (SUBLANES, LANES))
        return carry

    lax.fori_loop(0, MLSTM_BLOCK // L, chunk, 0)


def _mlstm(qm, km, vm, om, gates, gates_t, b_col, b_row, norm_w):
    b, s, _ = qm.shape
    tb = MLSTM_BLOCK
    blk = lambda width: pl.BlockSpec((1, tb, width), lambda bi, si: (bi, si, 0))
    return pl.pallas_call(
        _mlstm_kernel,
        out_shape=jax.ShapeDtypeStruct((b, s, MLSTM_WIDTH), BF16),
        grid=(b, s // tb),
        in_specs=[blk(MLSTM_WIDTH), blk(MLSTM_WIDTH), blk(MLSTM_WIDTH), blk(MLSTM_WIDTH),
                  blk(2 * MLSTM_HEADS),
                  pl.BlockSpec((1, tb // MLSTM_CHUNK, 2 * MLSTM_HEADS, MLSTM_CHUNK),
                               lambda bi, si: (bi, si, 0, 0)),
                  _const_spec(b_col.shape), _const_spec(b_row.shape), _const_spec(norm_w.shape)],
        out_specs=blk(MLSTM_WIDTH),
        scratch_shapes=[pltpu.VMEM((MLSTM_HEADS, HEAD_DIM, 2 * HEAD_DIM), F32),
                        pltpu.VMEM((MLSTM_HEADS, SUBLANES, LANES), F32)],
        compiler_params=pltpu.CompilerParams(dimension_semantics=("parallel", "arbitrary"),
                                             vmem_limit_bytes=VMEM_LIMIT_BYTES),
        name="mlstm",
    )(qm, km, vm, om, gates, gates_t, b_col, b_row, norm_w)


def _attn_kernel(q_ref, k_ref, v_ref, o_ref):
    t = ATTN_TILE
    qi = pl.program_id(1)
    diag = (lax.broadcasted_iota(jnp.int32, (t, t), 0) >= lax.broadcasted_iota(jnp.int32, (t, t), 1))
    neg_inf = jnp.float32(-jnp.inf)

    for h in range(MLA_HEADS):
        q = q_ref[0, :, h * QK_HEAD:(h + 1) * QK_HEAD]

        def step(j, carry, masked, h=h, q=q):
            m, l, acc = carry
            r0 = pl.multiple_of(j * t, t)
            k = k_ref[0, pl.ds(r0, t), h * QK_HEAD:(h + 1) * QK_HEAD]
            v = v_ref[0, pl.ds(r0, t), h * HEAD_DIM:(h + 1) * HEAD_DIM]
            s = lax.dot_general(q, k, (((1,), (1,)), ((), ())), preferred_element_type=F32)
            if masked:
                s = jnp.where(diag, s, neg_inf)
            m_new = jnp.maximum(m, jnp.max(s, axis=-1, keepdims=True))
            alpha = jnp.exp(m - m_new)
            p = jnp.exp(s - m_new)
            l = alpha * l + jnp.sum(p, axis=-1, keepdims=True)
            acc = alpha * acc + jnp.dot(p.astype(BF16), v, preferred_element_type=F32)
            return m_new, l, acc

        init = (jnp.full((t, 1), neg_inf, F32), jnp.zeros((t, 1), F32), jnp.zeros((t, HEAD_DIM), F32))
        carry = lax.fori_loop(0, qi, functools.partial(step, masked=False), init)
        _, l, acc = step(qi, carry, masked=True)
        o_ref[0, :, h * HEAD_DIM:(h + 1) * HEAD_DIM] = (acc / l).astype(BF16)


def _attention(qa, ka, va):
    b, s, _ = qa.shape
    t = ATTN_TILE
    return pl.pallas_call(
        _attn_kernel,
        out_shape=jax.ShapeDtypeStruct((b, s, MLA_WIDTH), BF16),
        grid=(b, s // t),
        in_specs=[pl.BlockSpec((1, t, MLA_HEADS * QK_HEAD), lambda bi, qi: (bi, qi, 0)),
                  pl.BlockSpec((1, s, MLA_HEADS * QK_HEAD), lambda bi, qi: (bi, 0, 0)),
                  pl.BlockSpec((1, s, MLA_WIDTH), lambda bi, qi: (bi, 0, 0))],
        out_specs=pl.BlockSpec((1, t, MLA_WIDTH), lambda bi, qi: (bi, qi, 0)),
        compiler_params=pltpu.CompilerParams(dimension_semantics=("parallel", "arbitrary"),
                                             vmem_limit_bytes=VMEM_LIMIT_BYTES),
        name="mla_attention",
    )(qa, ka, va)


def _outproj_kernel(ym_ref, ya_ref, x_ref, w_ref, g_ref, b_ref, o_ref):
    y = (jnp.dot(ym_ref[...], w_ref[0:MLSTM_WIDTH, :], preferred_element_type=F32)
         + jnp.dot(ya_ref[...], w_ref[MLSTM_WIDTH:, :], preferred_element_type=F32))
    o_ref[...] = _layer_norm(DEEPNORM_ALPHA * x_ref[...] + y, g_ref[...], b_ref[...])


def _outproj(ym2, ya2, x2, w_out, g, b):
    t = x2.shape[0]
    tm = ROW_TILE
    row = lambda width: pl.BlockSpec((tm, width), lambda i: (i, 0))
    return pl.pallas_call(
        _outproj_kernel,
        out_shape=jax.ShapeDtypeStruct((t, D_MODEL), F32),
        grid=(t // tm,),
        in_specs=[row(MLSTM_WIDTH), row(MLA_WIDTH), row(D_MODEL), _const_spec(w_out.shape),
                  _const_spec(g.shape), _const_spec(b.shape)],
        out_specs=row(D_MODEL),
        compiler_params=pltpu.CompilerParams(dimension_semantics=("parallel",),
                                             vmem_limit_bytes=VMEM_LIMIT_BYTES),
        name="outproj_ln",
    )(ym2, ya2, x2, w_out, g, b)


def _shift_rows(u, tail, k, row8):
    rolled = pltpu.roll(u, k, 0)
    top = jnp.where(row8 < k, pltpu.roll(tail, k, 0), rolled[0:SUBLANES])
    return jnp.concatenate([top, rolled[SUBLANES:]], axis=0)


def _ffn_kernel(x_ref, wup_ref, cw_ref, cb_ref, wdn_ref, g_ref, b_ref, o_ref, tail_ref, h_ref):
    tm = ROW_TILE
    tf = FFN_COLS

    @pl.when(pl.program_id(1) == 0)
    def _():
        tail_ref[...] = jnp.zeros_like(tail_ref)

    x = x_ref[0]
    xb = x.astype(BF16)
    row8 = lax.broadcasted_iota(jnp.int32, (SUBLANES, tf), 0)

    def conv_cols(lo):
        u = jnp.dot(xb, wup_ref[:, lo:lo + tf], preferred_element_type=F32)
        tail = tail_ref[:, lo:lo + tf]
        tail_ref[:, lo:lo + tf] = u[tm - SUBLANES:, :]
        return (_shift_rows(u, tail, 2, row8) * cw_ref[0:1, lo:lo + tf]
                + _shift_rows(u, tail, 1, row8) * cw_ref[1:2, lo:lo + tf]
                + u * cw_ref[2:3, lo:lo + tf] + cb_ref[:, lo:lo + tf])

    for c in range(FFN_DIM // tf):
        gate = conv_cols(c * tf)
        val = conv_cols(FFN_DIM + c * tf)
        h_ref[:, c * tf:(c + 1) * tf] = (gate * jax.nn.sigmoid(gate) * val).astype(BF16)

    y = jnp.dot(h_ref[...], wdn_ref[...], preferred_element_type=F32)
    o_ref[0] = _layer_norm(DEEPNORM_ALPHA * x + y, g_ref[...], b_ref[...])


def _ffn(x3, w_up, conv_w, conv_b, w_down, g, b):
    bsz, s, _ = x3.shape
    tm = ROW_TILE
    blk = pl.BlockSpec((1, tm, D_MODEL), lambda bi, si: (bi, si, 0))
    return pl.pallas_call(
        _ffn_kernel,
        out_shape=jax.ShapeDtypeStruct(x3.shape, F32),
        grid=(bsz, s // tm),
        in_specs=[blk, _const_spec(w_up.shape), _const_spec(conv_w.shape), _const_spec(conv_b.shape),
                  _const_spec(w_down.shape), _const_spec(g.shape), _const_spec(b.shape)],
        out_specs=blk,
        scratch_shapes=[pltpu.VMEM((SUBLANES, 2 * FFN_DIM), F32),
                        pltpu.VMEM((tm, FFN_DIM), BF16)],
        compiler_params=pltpu.CompilerParams(dimension_semantics=("parallel", "arbitrary"),
                                             vmem_limit_bytes=VMEM_LIMIT_BYTES),
        name="conv_ffn_ln",
    )(x3, w_up, conv_w, conv_b, w_down, g, b)


def _pool_kernel(x_ref, pw_ref, ls_ref, g_ref, b_ref, o_ref, xbuf_ref):
    tm = ROW_TILE
    si = pl.program_id(1)

    @pl.when(si == 0)
    def _():
        xbuf_ref[0:POOL_HALO, :] = jnp.zeros((POOL_HALO, D_MODEL), F32)

    x = x_ref[0]
    xbuf_ref[POOL_HALO:, :] = x
    t_pos = si * tm + lax.broadcasted_iota(jnp.int32, (tm, 1), 0)
    ys = []
    for gi, w in enumerate(POOL_WINDOWS):
        cols = slice(gi * POOL_GROUP_DIM, (gi + 1) * POOL_GROUP_DIM)
        acc = x[:, cols]
        for j in range(1, w):
            acc = acc + xbuf_ref[pl.ds(POOL_HALO - j, tm), cols]
        cnt = jnp.minimum(t_pos + 1, w).astype(F32)
        pooled = (acc / cnt - x[:, cols]).astype(BF16)
        ys.append(jnp.dot(pooled, pw_ref[gi], preferred_element_type=F32))
    y = jnp.concatenate(ys, axis=1) * ls_ref[...]
    o_ref[0] = _layer_norm(DEEPNORM_ALPHA * x + y, g_ref[...], b_ref[...])
    xbuf_ref[0:POOL_HALO, :] = x[tm - POOL_HALO:, :]


def _pool(x3, pool_w, layer_scale, g, b):
    bsz, s, _ = x3.shape
    tm = ROW_TILE
    blk = pl.BlockSpec((1, tm, D_MODEL), lambda bi, si: (bi, si, 0))
    return pl.pallas_call(
        _pool_kernel,
        out_shape=jax.ShapeDtypeStruct(x3.shape, F32),
        grid=(bsz, s // tm),
        in_specs=[blk, _const_spec(pool_w.shape), _const_spec(layer_scale.shape),
                  _const_spec(g.shape), _const_spec(b.shape)],
        out_specs=blk,
        scratch_shapes=[pltpu.VMEM((POOL_HALO + tm, D_MODEL), F32)],
        compiler_params=pltpu.CompilerParams(dimension_semantics=("parallel", "arbitrary"),
                                             vmem_limit_bytes=VMEM_LIMIT_BYTES),
        name="pool_ln",
    )(x3, pool_w, layer_scale, g, b)


def _rope_block_cols(w_rope):
    z = lambda n: jnp.zeros(w_rope.shape[:-1] + (n,), w_rope.dtype)
    return jnp.concatenate([z(ROPE_LO), w_rope[..., :ROPE_HALF], z(ROPE_HI - ROPE_LO - ROPE_HALF),
                            w_rope[..., ROPE_HALF:], z(LANES - ROPE_HI - ROPE_HALF)], axis=-1)


def _prep_w_in(w_in):
    offs = np.cumsum((0,) + IN_SIZES)
    parts = [w_in[:, offs[i]:offs[i + 1]] for i in range(len(IN_SIZES))]
    q_m, k_m, v_m, o_m, i_g, f_g, c_q, c_kv, k_r = parts
    gk = _rope_block_cols(k_r).at[:, 0:2 * MLSTM_HEADS].set(jnp.concatenate([i_g, f_g], axis=1))
    return jnp.concatenate([q_m, k_m, v_m, o_m, c_q, c_kv, gk], axis=1).astype(BF16)


def _prep_w_uq(w_uq):
    w = w_uq.reshape(Q_LORA, MLA_HEADS, HEAD_DIM + ROPE_DIM)
    blk = jnp.concatenate([w[..., :HEAD_DIM], _rope_block_cols(w[..., HEAD_DIM:])], axis=-1)
    return blk.reshape(Q_LORA, MLA_HEADS * QK_HEAD).astype(BF16)


def _prep_w_ukv(w_ukv):
    w = w_ukv.reshape(KV_LORA, MLA_HEADS, 2 * HEAD_DIM)
    k_nope = w[..., :HEAD_DIM].reshape(KV_LORA, MLA_WIDTH)
    v = w[..., HEAD_DIM:].reshape(KV_LORA, MLA_WIDTH)
    return jnp.concatenate([k_nope, v], axis=1).astype(BF16)


def _rope_lane_tables():
    inv_freq = ROPE_THETA ** (-jnp.arange(0, ROPE_DIM, 2, dtype=F32) / ROPE_DIM)
    freq = _rope_block_cols(jnp.concatenate([inv_freq, inv_freq])[None, :])
    ones = jnp.ones((1, ROPE_HALF), F32)
    cmask = _rope_block_cols(jnp.concatenate([ones, ones], axis=1))
    smask = _rope_block_cols(jnp.concatenate([-ones, ones], axis=1))
    return freq, cmask, smask


def kernel(x, positions, even_w_in, even_b_igate, even_b_fgate, even_mlstm_norm, even_q_norm,
           even_kv_norm, even_w_uq, even_w_ukv, even_w_out, odd_pool_w, odd_layer_scale,
           ffn_w_up, ffn_conv_w, ffn_conv_b, ffn_w_down, ln_mix_g, ln_mix_b, ln_ffn_g, ln_ffn_b):
    bsz, s, d = x.shape
    t = bsz * s
    row = lambda v: v.reshape(1, -1)

    freq, cmask, smask = _rope_lane_tables()
    qm, km, vm, om, gates, qa, ka, va = _inproj(
        x.reshape(t, d), positions.astype(F32).reshape(t, 1), _prep_w_in(even_w_in[0]),
        row(even_q_norm[0]), row(even_kv_norm[0]), _prep_w_uq(even_w_uq[0]), _prep_w_ukv(even_w_ukv[0]),
        freq, cmask, smask)

    gate_bias = jnp.concatenate([even_b_igate[0], even_b_fgate[0]])
    gates3 = gates.reshape(bsz, s, 2 * MLSTM_HEADS)
    gates_t = gates3.reshape(bsz, s // MLSTM_CHUNK, MLSTM_CHUNK, 2 * MLSTM_HEADS).transpose(0, 1, 3, 2)
    b3 = lambda a: a.reshape(bsz, s, a.shape[-1])
    ym = _mlstm(b3(qm), b3(km), b3(vm), b3(om), gates3, gates_t,
                gate_bias.reshape(1, -1), gate_bias.reshape(-1, 1), row(even_mlstm_norm[0]))
    ya = _attention(b3(qa), b3(ka), b3(va))

    x1 = _outproj(ym.reshape(t, MLSTM_WIDTH), ya.reshape(t, MLA_WIDTH), x.reshape(t, d),
                  even_w_out[0].astype(BF16), row(ln_mix_g[0]), row(ln_mix_b[0]))
    x1 = _ffn(x1.reshape(bsz, s, d), ffn_w_up[0].astype(BF16), ffn_conv_w[0], row(ffn_conv_b[0]),
              ffn_w_down[0].astype(BF16), row(ln_ffn_g[0]), row(ln_ffn_b[0]))

    x2 = _pool(x1, odd_pool_w[0].astype(BF16), row(odd_layer_scale[0]), row(ln_mix_g[1]), row(ln_mix_b[1]))
    return _ffn(x2, ffn_w_up[1].astype(BF16), ffn_conv_w[1], row(ffn_conv_b[1]),
                ffn_w_down[1].astype(BF16), row(ln_ffn_g[1]), row(ln_ffn_b[1]))
```

```python
import functools

import jax
import jax.numpy as jnp
import numpy as np
from jax import lax
from jax.experimental import pallas as pl
from jax.experimental.pallas import tpu as pltpu

F32 = jnp.float32
BF16 = jnp.bfloat16

D_MODEL = 1024
DEPTH = 2
MLSTM_HEADS = 4
HEAD_DIM = 128
MLSTM_WIDTH = MLSTM_HEADS * HEAD_DIM
MLA_HEADS = 4
ROPE_DIM = 64
ROPE_HALF = ROPE_DIM // 2
Q_LORA = 256
KV_LORA = 128
MLA_WIDTH = MLA_HEADS * HEAD_DIM
ROPE_THETA = 10000.0
POOL_WINDOWS = (2, 4, 8, 16)
POOL_GROUP_DIM = D_MODEL // len(POOL_WINDOWS)
FFN_DIM = 2816
CONV_WIDTH = 3
LN_EPS = 1e-5
RMS_EPS = 1e-6
DEEPNORM_ALPHA = (2 * DEPTH) ** 0.25
IN_SIZES = (MLSTM_WIDTH, MLSTM_WIDTH, MLSTM_WIDTH, MLSTM_WIDTH, MLSTM_HEADS, MLSTM_HEADS,
            Q_LORA, KV_LORA, ROPE_DIM)

LANES = 128
SUBLANES = 8
MXU_DIM = 256
VMEM_LIMIT_BYTES = 56 * 1024 * 1024

ROPE_LO = 8
ROPE_HI = ROPE_LO + LANES // 2
QK_HEAD = 2 * LANES

ROW_TILE = 512
MLSTM_CHUNK = 128
MLSTM_BLOCK = 512
ATTN_TILE = 512
FFN_COLS = 256
POOL_HALO = 16


def _const_spec(shape):
    nd = len(shape)
    return pl.BlockSpec(shape, lambda *_: (0,) * nd, pipeline_mode=pl.Buffered(1))


def _layer_norm(z, g, b):
    mu = jnp.mean(z, axis=-1, keepdims=True)
    d = z - mu
    var = jnp.mean(d * d, axis=-1, keepdims=True)
    return d * lax.rsqrt(var + LN_EPS) * g + b


def _rms_norm(z, g):
    return z * lax.rsqrt(jnp.mean(z * z, axis=-1, keepdims=True) + RMS_EPS) * g


def _log_sigmoid(z):
    return -(jnp.maximum(-z, 0.0) + jnp.log1p(jnp.exp(-jnp.abs(z))))


def _split3(z):
    hi = z.astype(BF16)
    r1 = z - hi.astype(F32)
    mid = r1.astype(BF16)
    lo = (r1 - mid.astype(F32)).astype(BF16)
    return hi, mid, lo


def _inproj_kernel(x_ref, pos_ref, w_in_ref, qn_ref, kvn_ref, wuq_ref, wukv_ref, freq_ref, cmask_ref,
                   smask_ref, qm_ref, km_ref, vm_ref, om_ref, gates_ref, qa_ref, ka_ref, va_ref):
    xb = x_ref[...].astype(BF16)

    def proj(lo, width):
        return jnp.dot(xb, w_in_ref[:, lo:lo + width], preferred_element_type=F32)

    w = MLSTM_WIDTH
    qm_ref[...] = proj(0, w).astype(BF16)
    km_ref[...] = (proj(w, w) * (HEAD_DIM ** -0.5)).astype(BF16)
    vm_ref[...] = proj(2 * w, w).astype(BF16)
    om_ref[...] = proj(3 * w, w)

    ang = pos_ref[...] * freq_ref[...]
    cosm = jnp.cos(ang) * cmask_ref[...]
    sinm = jnp.sin(ang) * smask_ref[...]

    def rope(blk):
        return blk * cosm + pltpu.roll(blk, LANES // 2, 1) * sinm

    c_q = proj(4 * w, Q_LORA)
    c_kv = proj(4 * w + Q_LORA, KV_LORA)
    gk = proj(4 * w + Q_LORA + KV_LORA, LANES)
    gates_ref[...] = gk[:, 0:2 * MLSTM_HEADS]
    k_rope = rope(gk).astype(BF16)

    scale = (HEAD_DIM + ROPE_DIM) ** -0.5 * float(np.log2(np.e))
    q = jnp.dot(_rms_norm(c_q, qn_ref[...]).astype(BF16), wuq_ref[...], preferred_element_type=F32)
    kv = jnp.dot(_rms_norm(c_kv, kvn_ref[...]).astype(BF16), wukv_ref[...], preferred_element_type=F32)
    for h in range(MLA_HEADS):
        lo = h * QK_HEAD
        qa_ref[:, lo:lo + LANES] = (q[:, lo:lo + LANES] * scale).astype(BF16)
        qa_ref[:, lo + LANES:lo + QK_HEAD] = (rope(q[:, lo + LANES:lo + QK_HEAD]) * scale).astype(BF16)
        ka_ref[:, lo:lo + LANES] = kv[:, h * HEAD_DIM:(h + 1) * HEAD_DIM].astype(BF16)
        ka_ref[:, lo + LANES:lo + QK_HEAD] = k_rope
    va_ref[...] = kv[:, MLA_WIDTH:].astype(BF16)


def _inproj(x2, pos2, w_in_p, q_norm, kv_norm, wuq_p, wukv_p, freq, cmask, smask):
    t = x2.shape[0]
    tm = ROW_TILE
    row = lambda width: pl.BlockSpec((tm, width), lambda i: (i, 0))
    out_shapes = (
        jax.ShapeDtypeStruct((t, MLSTM_WIDTH), BF16),
        jax.ShapeDtypeStruct((t, MLSTM_WIDTH), BF16),
        jax.ShapeDtypeStruct((t, MLSTM_WIDTH), BF16),
        jax.ShapeDtypeStruct((t, MLSTM_WIDTH), F32),
        jax.ShapeDtypeStruct((t, 2 * MLSTM_HEADS), F32),
        jax.ShapeDtypeStruct((t, MLA_HEADS * QK_HEAD), BF16),
        jax.ShapeDtypeStruct((t, MLA_HEADS * QK_HEAD), BF16),
        jax.ShapeDtypeStruct((t, MLA_WIDTH), BF16),
    )
    return pl.pallas_call(
        _inproj_kernel,
        out_shape=out_shapes,
        grid=(t // tm,),
        in_specs=[row(D_MODEL), row(1), _const_spec(w_in_p.shape), _const_spec(q_norm.shape),
                  _const_spec(kv_norm.shape), _const_spec(wuq_p.shape), _const_spec(wukv_p.shape),
                  _const_spec(freq.shape), _const_spec(cmask.shape), _const_spec(smask.shape)],
        out_specs=tuple(row(s.shape[1]) for s in out_shapes),
        compiler_params=pltpu.CompilerParams(dimension_semantics=("parallel",),
                                             vmem_limit_bytes=VMEM_LIMIT_BYTES),
        name="inproj",
    )(x2, pos2, w_in_p, q_norm, kv_norm, wuq_p, wukv_p, freq, cmask, smask)


def _mlstm_kernel(q_ref, k_ref, v_ref, o_ref, g_ref, gt_ref, bcol_ref, brow_ref, nw_ref,
                  y_ref, ct_ref, m_ref):
    L = MLSTM_CHUNK
    d = HEAD_DIM

    @pl.when(pl.program_id(1) == 0)
    def _():
        ct_ref[...] = jnp.zeros_like(ct_ref)
        m_ref[...] = jnp.zeros_like(m_ref)

    r_idx = lax.broadcasted_iota(jnp.int32, (L, L), 0)
    c_idx = lax.broadcasted_iota(jnp.int32, (L, L), 1)
    causal = r_idx >= c_idx
    tri = causal.astype(BF16)
    tri_t = (r_idx <= c_idx).astype(BF16)
    ones_blk = jnp.ones((L, d), BF16)
    neg_inf = jnp.float32(-jnp.inf)

    def chunk(c, carry):
        r0 = pl.multiple_of(c * L, L)
        g_col = g_ref[0, pl.ds(r0, L), :] + bcol_ref[...]
        g_row = gt_ref[0, c] + brow_ref[...]
        ls_col = _log_sigmoid(g_col)
        ls_row = _log_sigmoid(g_row)
        b_col = sum(jnp.dot(tri, p, preferred_element_type=F32) for p in _split3(ls_col))
        b_row = sum(jnp.dot(p, tri_t, preferred_element_type=F32) for p in _split3(ls_row))

        for h in range(MLSTM_HEADS):
            hs = slice(h * d, (h + 1) * d)
            fh = MLSTM_HEADS + h
            q = q_ref[0, pl.ds(r0, L), hs]
            k = k_ref[0, pl.ds(r0, L), hs]
            v = v_ref[0, pl.ds(r0, L), hs]
            a_col = g_col[:, h:h + 1] - b_col[:, fh:fh + 1]
            a_row = g_row[h:h + 1, :] - b_row[fh:fh + 1, :]
            bt_col = b_col[:, fh:fh + 1]
            b_last = b_row[fh:fh + 1, L - 1:L]
            m_prev = m_ref[h][0:1, 0:1]

            m_col = jnp.maximum(
                jnp.max(jnp.where(causal, a_row, neg_inf), axis=-1, keepdims=True), m_prev)
            wgt = jnp.exp(jnp.where(causal, a_row - m_col, neg_inf))
            s = lax.dot_general(q, k, (((1,), (1,)), ((), ())), preferred_element_type=F32)
            a_mat = (s * wgt).astype(BF16)
            v_aug = jnp.concatenate([v, ones_blk], axis=1)
            ct = ct_ref[h]
            tot = (jnp.dot(a_mat, v_aug, preferred_element_type=F32)
                   + jnp.exp(m_prev - m_col) * jnp.dot(q, ct.astype(BF16), preferred_element_type=F32))
            num = tot[:, :d]
            den = tot[:, d:]
            hraw = num / jnp.maximum(jnp.abs(den), jnp.exp(-(bt_col + m_col)))
            hn = _rms_norm(hraw, nw_ref[:, hs])
            gate = jax.nn.sigmoid(o_ref[0, pl.ds(r0, L), hs])
            y_ref[0, pl.ds(r0, L), hs] = (hn * gate).astype(BF16)

            m_last = m_col[L - 1:L, :]
            w_col = jnp.exp(a_col - m_last)
            wv = (v_aug.astype(F32) * w_col).astype(BF16)
            upd = lax.dot_general(k, wv, (((0,), (0,)), ((), ())), preferred_element_type=F32)
            ct_ref[h] = jnp.exp(m_prev - m_last) * ct + upd
            m_ref[h] = jnp.broadcast_to(b_last + m_last, (SUBLANES, LANES))
        return carry

    lax.fori_loop(0, MLSTM_BLOCK // L, chunk, 0)


def _mlstm(qm, km, vm, om, gates, gates_t, b_col, b_row, norm_w):
    b, s, _ = qm.shape
    tb = MLSTM_BLOCK
    blk = lambda width: pl.BlockSpec((1, tb, width), lambda bi, si: (bi, si, 0))
    return pl.pallas_call(
        _mlstm_kernel,
        out_shape=jax.ShapeDtypeStruct((b, s, MLSTM_WIDTH), BF16),
        grid=(b, s // tb),
        in_specs=[blk(MLSTM_WIDTH), blk(MLSTM_WIDTH), blk(MLSTM_WIDTH), blk(MLSTM_WIDTH),
                  blk(2 * MLSTM_HEADS),
                  pl.BlockSpec((1, tb // MLSTM_CHUNK, 2 * MLSTM_HEADS, MLSTM_CHUNK),
                               lambda bi, si: (bi, si, 0, 0)),
                  _const_spec(b_col.shape), _const_spec(b_row.shape), _const_spec(norm_w.shape)],
        out_specs=blk(MLSTM_WIDTH),
        scratch_shapes=[pltpu.VMEM((MLSTM_HEADS, HEAD_DIM, 2 * HEAD_DIM), F32),
                        pltpu.VMEM((MLSTM_HEADS, SUBLANES, LANES), F32)],
        compiler_params=pltpu.CompilerParams(dimension_semantics=("parallel", "arbitrary"),
                                             vmem_limit_bytes=VMEM_LIMIT_BYTES),
        name="mlstm",
    )(qm, km, vm, om, gates, gates_t, b_col, b_row, norm_w)


def _attn_kernel(q_ref, k_ref, v_ref, o_ref, s_ref, m_ref, acc_ref):
    t = ATTN_TILE
    nh = MLA_HEADS
    qi = pl.program_id(1)
    diag = (lax.broadcasted_iota(jnp.int32, (t, t), 0) >= lax.broadcasted_iota(jnp.int32, (t, t), 1))
    neg_inf = jnp.float32(-jnp.inf)
    ones_blk = jnp.ones((t, HEAD_DIM), BF16)
    m_ref[...] = jnp.full(m_ref.shape, neg_inf, F32)
    acc_ref[...] = jnp.zeros_like(acc_ref)

    def scores(j, slot, h):
        r0 = pl.multiple_of(j * t, t)
        q = q_ref[0, :, h * QK_HEAD:(h + 1) * QK_HEAD]
        k = k_ref[0, pl.ds(r0, t), h * QK_HEAD:(h + 1) * QK_HEAD]
        s_ref[slot * nh + h] = lax.dot_general(q, k, (((1,), (1,)), ((), ())),
                                               preferred_element_type=F32)

    def accumulate(j, slot, h, masked):
        r0 = pl.multiple_of(j * t, t)
        v_aug = jnp.concatenate(
            [v_ref[0, pl.ds(r0, t), h * HEAD_DIM:(h + 1) * HEAD_DIM], ones_blk], axis=1)
        s = s_ref[slot * nh + h]
        if masked:
            s = jnp.where(diag, s, neg_inf)
        m_old = m_ref[h]
        m_new = jnp.maximum(m_old, jnp.max(s, axis=-1, keepdims=True))
        p = jnp.exp2(s - jnp.concatenate([m_new] * (t // LANES), axis=1)).astype(BF16)
        alpha = jnp.exp2(m_old - m_new)
        acc_ref[h] = (jnp.concatenate([alpha, alpha], axis=1) * acc_ref[h]
                      + jnp.dot(p, v_aug, preferred_element_type=F32))
        m_ref[h] = m_new

    for h in range(nh):
        scores(0, 0, h)

    def half_step(j, slot):
        for h in range(nh):
            scores(j + 1, 1 - slot, h)
            accumulate(j, slot, h, masked=False)

    def body(i, carry):
        half_step(2 * i, 0)

        @pl.when(2 * i + 1 < qi)
        def _():
            half_step(2 * i + 1, 1)
        return carry

    lax.fori_loop(0, (qi + 1) // 2, body, 0)
    for h in range(nh):
        accumulate(qi, lax.rem(qi, 2), h, masked=True)
        acc = acc_ref[h]
        o_ref[0, :, h * HEAD_DIM:(h + 1) * HEAD_DIM] = (acc[:, :HEAD_DIM] / acc[:, HEAD_DIM:]).astype(BF16)


def _attention(qa, ka, va):
    b, s, _ = qa.shape
    t = ATTN_TILE
    return pl.pallas_call(
        _attn_kernel,
        out_shape=jax.ShapeDtypeStruct((b, s, MLA_WIDTH), BF16),
        grid=(b, s // t),
        in_specs=[pl.BlockSpec((1, t, MLA_HEADS * QK_HEAD), lambda bi, qi: (bi, qi, 0)),
                  pl.BlockSpec((1, s, MLA_HEADS * QK_HEAD), lambda bi, qi: (bi, 0, 0)),
                  pl.BlockSpec((1, s, MLA_WIDTH), lambda bi, qi: (bi, 0, 0))],
        out_specs=pl.BlockSpec((1, t, MLA_WIDTH), lambda bi, qi: (bi, qi, 0)),
        scratch_shapes=[pltpu.VMEM((2 * MLA_HEADS, t, t), F32),
                        pltpu.VMEM((MLA_HEADS, t, LANES), F32),
                        pltpu.VMEM((MLA_HEADS, t, 2 * HEAD_DIM), F32)],
        compiler_params=pltpu.CompilerParams(dimension_semantics=("parallel", "arbitrary"),
                                             vmem_limit_bytes=VMEM_LIMIT_BYTES),
        name="mla_attention",
    )(qa, ka, va)


def _outproj_kernel(ym_ref, ya_ref, x_ref, w_ref, g_ref, b_ref, o_ref):
    y = (jnp.dot(ym_ref[...], w_ref[0:MLSTM_WIDTH, :], preferred_element_type=F32)
         + jnp.dot(ya_ref[...], w_ref[MLSTM_WIDTH:, :], preferred_element_type=F32))
    o_ref[...] = _layer_norm(DEEPNORM_ALPHA * x_ref[...] + y, g_ref[...], b_ref[...])


def _outproj(ym2, ya2, x2, w_out, g, b):
    t = x2.shape[0]
    tm = ROW_TILE
    row = lambda width: pl.BlockSpec((tm, width), lambda i: (i, 0))
    return pl.pallas_call(
        _outproj_kernel,
        out_shape=jax.ShapeDtypeStruct((t, D_MODEL), F32),
        grid=(t // tm,),
        in_specs=[row(MLSTM_WIDTH), row(MLA_WIDTH), row(D_MODEL), _const_spec(w_out.shape),
                  _const_spec(g.shape), _const_spec(b.shape)],
        out_specs=row(D_MODEL),
        compiler_params=pltpu.CompilerParams(dimension_semantics=("parallel",),
                                             vmem_limit_bytes=VMEM_LIMIT_BYTES),
        name="outproj_ln",
    )(ym2, ya2, x2, w_out, g, b)


def _shift_rows(u, tail, k, row8):
    rolled = pltpu.roll(u, k, 0)
    top = jnp.where(row8 < k, pltpu.roll(tail, k, 0), rolled[0:SUBLANES])
    return jnp.concatenate([top, rolled[SUBLANES:]], axis=0)


def _ffn_kernel(x_ref, wup_ref, cw_ref, cb_ref, wdn_ref, g_ref, b_ref, o_ref, tail_ref, h_ref):
    tm = ROW_TILE
    tf = FFN_COLS

    @pl.when(pl.program_id(1) == 0)
    def _():
        tail_ref[...] = jnp.zeros_like(tail_ref)

    x = x_ref[0]
    xb = x.astype(BF16)
    row8 = lax.broadcasted_iota(jnp.int32, (SUBLANES, tf), 0)

    def conv_cols(lo):
        u = jnp.dot(xb, wup_ref[:, lo:lo + tf], preferred_element_type=F32)
        tail = tail_ref[:, lo:lo + tf]
        tail_ref[:, lo:lo + tf] = u[tm - SUBLANES:, :]
        return (_shift_rows(u, tail, 2, row8) * cw_ref[0:1, lo:lo + tf]
                + _shift_rows(u, tail, 1, row8) * cw_ref[1:2, lo:lo + tf]
                + u * cw_ref[2:3, lo:lo + tf] + cb_ref[:, lo:lo + tf])

    for c in range(FFN_DIM // tf):
        gate = conv_cols(c * tf)
        val = conv_cols(FFN_DIM + c * tf)
        h_ref[:, c * tf:(c + 1) * tf] = (gate * jax.nn.sigmoid(gate) * val).astype(BF16)

    y = jnp.dot(h_ref[...], wdn_ref[...], preferred_element_type=F32)
    o_ref[0] = _layer_norm(DEEPNORM_ALPHA * x + y, g_ref[...], b_ref[...])


def _ffn(x3, w_up, conv_w, conv_b, w_down, g, b):
    bsz, s, _ = x3.shape
    tm = ROW_TILE
    blk = pl.BlockSpec((1, tm, D_MODEL), lambda bi, si: (bi, si, 0))
    return pl.pallas_call(
        _ffn_kernel,
        out_shape=jax.ShapeDtypeStruct(x3.shape, F32),
        grid=(bsz, s // tm),
        in_specs=[blk, _const_spec(w_up.shape), _const_spec(conv_w.shape), _const_spec(conv_b.shape),
                  _const_spec(w_down.shape), _const_spec(g.shape), _const_spec(b.shape)],
        out_specs=blk,
        scratch_shapes=[pltpu.VMEM((SUBLANES, 2 * FFN_DIM), F32),
                        pltpu.VMEM((tm, FFN_DIM), BF16)],
        compiler_params=pltpu.CompilerParams(dimension_semantics=("parallel", "arbitrary"),
                                             vmem_limit_bytes=VMEM_LIMIT_BYTES),
        name="conv_ffn_ln",
    )(x3, w_up, conv_w, conv_b, w_down, g, b)


def _pool_kernel(x_ref, pw_ref, ls_ref, g_ref, b_ref, o_ref, xbuf_ref):
    tm = ROW_TILE
    si = pl.program_id(1)

    @pl.when(si == 0)
    def _():
        xbuf_ref[0:POOL_HALO, :] = jnp.zeros((POOL_HALO, D_MODEL), F32)

    x = x_ref[0]
    xbuf_ref[POOL_HALO:, :] = x
    t_pos = si * tm + lax.broadcasted_iota(jnp.int32, (tm, 1), 0)
    ys = []
    for gi, w in enumerate(POOL_WINDOWS):
        cols = slice(gi * POOL_GROUP_DIM, (gi + 1) * POOL_GROUP_DIM)
        acc = x[:, cols]
        for j in range(1, w):
            acc = acc + xbuf_ref[pl.ds(POOL_HALO - j, tm), cols]
        cnt = jnp.minimum(t_pos + 1, w).astype(F32)
        pooled = (acc / cnt - x[:, cols]).astype(BF16)
        ys.append(jnp.dot(pooled, pw_ref[gi], preferred_element_type=F32))
    y = jnp.concatenate(ys, axis=1) * ls_ref[...]
    o_ref[0] = _layer_norm(DEEPNORM_ALPHA * x + y, g_ref[...], b_ref[...])
    xbuf_ref[0:POOL_HALO, :] = x[tm - POOL_HALO:, :]


def _pool(x3, pool_w, layer_scale, g, b):
    bsz, s, _ = x3.shape
    tm = ROW_TILE
    blk = pl.BlockSpec((1, tm, D_MODEL), lambda bi, si: (bi, si, 0))
    return pl.pallas_call(
        _pool_kernel,
        out_shape=jax.ShapeDtypeStruct(x3.shape, F32),
        grid=(bsz, s // tm),
        in_specs=[blk, _const_spec(pool_w.shape), _const_spec(layer_scale.shape),
                  _const_spec(g.shape), _const_spec(b.shape)],
        out_specs=blk,
        scratch_shapes=[pltpu.VMEM((POOL_HALO + tm, D_MODEL), F32)],
        compiler_params=pltpu.CompilerParams(dimension_semantics=("parallel", "arbitrary"),
                                             vmem_limit_bytes=VMEM_LIMIT_BYTES),
        name="pool_ln",
    )(x3, pool_w, layer_scale, g, b)


def _rope_block_cols(w_rope):
    z = lambda n: jnp.zeros(w_rope.shape[:-1] + (n,), w_rope.dtype)
    return jnp.concatenate([z(ROPE_LO), w_rope[..., :ROPE_HALF], z(ROPE_HI - ROPE_LO - ROPE_HALF),
                            w_rope[..., ROPE_HALF:], z(LANES - ROPE_HI - ROPE_HALF)], axis=-1)


def _prep_w_in(w_in):
    offs = np.cumsum((0,) + IN_SIZES)
    parts = [w_in[:, offs[i]:offs[i + 1]] for i in range(len(IN_SIZES))]
    q_m, k_m, v_m, o_m, i_g, f_g, c_q, c_kv, k_r = parts
    gk = _rope_block_cols(k_r).at[:, 0:2 * MLSTM_HEADS].set(jnp.concatenate([i_g, f_g], axis=1))
    return jnp.concatenate([q_m, k_m, v_m, o_m, c_q, c_kv, gk], axis=1).astype(BF16)


def _prep_w_uq(w_uq):
    w = w_uq.reshape(Q_LORA, MLA_HEADS, HEAD_DIM + ROPE_DIM)
    blk = jnp.concatenate([w[..., :HEAD_DIM], _rope_block_cols(w[..., HEAD_DIM:])], axis=-1)
    return blk.reshape(Q_LORA, MLA_HEADS * QK_HEAD).astype(BF16)


def _prep_w_ukv(w_ukv):
    w = w_ukv.reshape(KV_LORA, MLA_HEADS, 2 * HEAD_DIM)
    k_nope = w[..., :HEAD_DIM].reshape(KV_LORA, MLA_WIDTH)
    v = w[..., HEAD_DIM:].reshape(KV_LORA, MLA_WIDTH)
    return jnp.concatenate([k_nope, v], axis=1).astype(BF16)


def _rope_lane_tables():
    inv_freq = ROPE_THETA ** (-jnp.arange(0, ROPE_DIM, 2, dtype=F32) / ROPE_DIM)
    freq = _rope_block_cols(jnp.concatenate([inv_freq, inv_freq])[None, :])
    ones = jnp.ones((1, ROPE_HALF), F32)
    cmask = _rope_block_cols(jnp.concatenate([ones, ones], axis=1))
    smask = _rope_block_cols(jnp.concatenate([-ones, ones], axis=1))
    return freq, cmask, smask


def kernel(x, positions, even_w_in, even_b_igate, even_b_fgate, even_mlstm_norm, even_q_norm,
           even_kv_norm, even_w_uq, even_w_ukv, even_w_out, odd_pool_w, odd_layer_scale,
           ffn_w_up, ffn_conv_w, ffn_conv_b, ffn_w_down, ln_mix_g, ln_mix_b, ln_ffn_g, ln_ffn_b):
    bsz, s, d = x.shape
    t = bsz * s
    row = lambda v: v.reshape(1, -1)

    freq, cmask, smask = _rope_lane_tables()
    qm, km, vm, om, gates, qa, ka, va = _inproj(
        x.reshape(t, d), positions.astype(F32).reshape(t, 1), _prep_w_in(even_w_in[0]),
        row(even_q_norm[0]), row(even_kv_norm[0]), _prep_w_uq(even_w_uq[0]), _prep_w_ukv(even_w_ukv[0]),
        freq, cmask, smask)

    gate_bias = jnp.concatenate([even_b_igate[0], even_b_fgate[0]])
    gates3 = gates.reshape(bsz, s, 2 * MLSTM_HEADS)
    gates_t = gates3.reshape(bsz, s // MLSTM_CHUNK, MLSTM_CHUNK, 2 * MLSTM_HEADS).transpose(0, 1, 3, 2)
    b3 = lambda a: a.reshape(bsz, s, a.shape[-1])
    ym = _mlstm(b3(qm), b3(km), b3(vm), b3(om), gates3, gates_t,
                gate_bias.reshape(1, -1), gate_bias.reshape(-1, 1), row(even_mlstm_norm[0]))
    ya = _attention(b3(qa), b3(ka), b3(va))

    x1 = _outproj(ym.reshape(t, MLSTM_WIDTH), ya.reshape(t, MLA_WIDTH), x.reshape(t, d),
                  even_w_out[0].astype(BF16), row(ln_mix_g[0]), row(ln_mix_b[0]))
    x1 = _ffn(x1.reshape(bsz, s, d), ffn_w_up[0].astype(BF16), ffn_conv_w[0], row(ffn_conv_b[0]),
              ffn_w_down[0].astype(BF16), row(ln_ffn_g[0]), row(ln_ffn_b[0]))

    x2 = _pool(x1, odd_pool_w[0].astype(BF16), row(odd_layer_scale[0]), row(ln_mix_g[1]), row(ln_mix_b[1]))
    return _ffn(x2, ffn_w_up[1].astype(BF16), ffn_conv_w[1], row(ffn_conv_b[1]),
                ffn_w_down[1].astype(BF16), row(ln_ffn_g[1]), row(ln_ffn_b[1]))
```

```python
import jax
import jax.numpy as jnp
import numpy as np
from jax import lax
from jax.experimental import pallas as pl
from jax.experimental.pallas import tpu as pltpu

F32 = jnp.float32
BF16 = jnp.bfloat16

D_MODEL = 1024
DEPTH = 2
MLSTM_HEADS = 4
HEAD_DIM = 128
MLSTM_WIDTH = MLSTM_HEADS * HEAD_DIM
MLA_HEADS = 4
ROPE_DIM = 64
ROPE_HALF = ROPE_DIM // 2
Q_LORA = 256
KV_LORA = 128
MLA_WIDTH = MLA_HEADS * HEAD_DIM
ROPE_THETA = 10000.0
POOL_WINDOWS = (2, 4, 8, 16)
POOL_GROUP_DIM = D_MODEL // len(POOL_WINDOWS)
FFN_DIM = 2816
CONV_WIDTH = 3
LN_EPS = 1e-5
RMS_EPS = 1e-6
DEEPNORM_ALPHA = (2 * DEPTH) ** 0.25
IN_SIZES = (MLSTM_WIDTH, MLSTM_WIDTH, MLSTM_WIDTH, MLSTM_WIDTH, MLSTM_HEADS, MLSTM_HEADS,
            Q_LORA, KV_LORA, ROPE_DIM)

LANES = 128
SUBLANES = 8
MXU_DIM = 256
VMEM_LIMIT_BYTES = 56 * 1024 * 1024

ROPE_LO = 8
ROPE_HI = ROPE_LO + LANES // 2
QK_HEAD = 2 * LANES

ROW_TILE = 512
MLSTM_CHUNK = 256
MLSTM_BLOCK = 1024
ATTN_TILE = 512
FFN_COLS = 256
FFN_ROW_TILE = 1024
POOL_HALO = 16


def _const_spec(shape):
    nd = len(shape)
    return pl.BlockSpec(shape, lambda *_: (0,) * nd, pipeline_mode=pl.Buffered(1))


def _layer_norm(z, g, b):
    mu = jnp.mean(z, axis=-1, keepdims=True)
    d = z - mu
    var = jnp.mean(d * d, axis=-1, keepdims=True)
    return d * lax.rsqrt(var + LN_EPS) * g + b


def _rms_norm(z, g):
    return z * lax.rsqrt(jnp.mean(z * z, axis=-1, keepdims=True) + RMS_EPS) * g


def _log_sigmoid(z):
    return -(jnp.maximum(-z, 0.0) + jnp.log1p(jnp.exp(-jnp.abs(z))))


def _split3(z):
    hi = z.astype(BF16)
    r1 = z - hi.astype(F32)
    mid = r1.astype(BF16)
    lo = (r1 - mid.astype(F32)).astype(BF16)
    return hi, mid, lo


def _inproj_kernel(x_ref, pos_ref, w_in_ref, qn_ref, kvn_ref, wuq_ref, wukv_ref, freq_ref, cmask_ref,
                   smask_ref, qm_ref, km_ref, vm_ref, om_ref, gates_ref, qa_ref, ka_ref, va_ref):
    xb = x_ref[...].astype(BF16)

    def proj(lo, width):
        return jnp.dot(xb, w_in_ref[:, lo:lo + width], preferred_element_type=F32)

    w = MLSTM_WIDTH
    qm_ref[...] = proj(0, w).astype(BF16)
    km_ref[...] = (proj(w, w) * (HEAD_DIM ** -0.5)).astype(BF16)
    vm_ref[...] = proj(2 * w, w).astype(BF16)
    om_ref[...] = proj(3 * w, w)

    ang = pos_ref[...] * freq_ref[...]
    cosm = jnp.cos(ang) * cmask_ref[...]
    sinm = jnp.sin(ang) * smask_ref[...]

    def rope(blk):
        return blk * cosm + pltpu.roll(blk, LANES // 2, 1) * sinm

    c_q = proj(4 * w, Q_LORA)
    c_kv = proj(4 * w + Q_LORA, KV_LORA)
    gk = proj(4 * w + Q_LORA + KV_LORA, LANES)
    gates_ref[...] = gk[:, 0:2 * MLSTM_HEADS]
    k_rope = rope(gk).astype(BF16)

    scale = (HEAD_DIM + ROPE_DIM) ** -0.5 * float(np.log2(np.e))
    q = jnp.dot(_rms_norm(c_q, qn_ref[...]).astype(BF16), wuq_ref[...], preferred_element_type=F32)
    kv = jnp.dot(_rms_norm(c_kv, kvn_ref[...]).astype(BF16), wukv_ref[...], preferred_element_type=F32)
    for h in range(MLA_HEADS):
        lo = h * QK_HEAD
        qa_ref[:, lo:lo + LANES] = (q[:, lo:lo + LANES] * scale).astype(BF16)
        qa_ref[:, lo + LANES:lo + QK_HEAD] = (rope(q[:, lo + LANES:lo + QK_HEAD]) * scale).astype(BF16)
        ka_ref[:, lo:lo + LANES] = kv[:, h * HEAD_DIM:(h + 1) * HEAD_DIM].astype(BF16)
        ka_ref[:, lo + LANES:lo + QK_HEAD] = k_rope
    va_ref[...] = kv[:, MLA_WIDTH:].astype(BF16)


def _inproj(x2, pos2, w_in_p, q_norm, kv_norm, wuq_p, wukv_p, freq, cmask, smask):
    t = x2.shape[0]
    tm = ROW_TILE
    row = lambda width: pl.BlockSpec((tm, width), lambda i: (i, 0))
    out_shapes = (
        jax.ShapeDtypeStruct((t, MLSTM_WIDTH), BF16),
        jax.ShapeDtypeStruct((t, MLSTM_WIDTH), BF16),
        jax.ShapeDtypeStruct((t, MLSTM_WIDTH), BF16),
        jax.ShapeDtypeStruct((t, MLSTM_WIDTH), F32),
        jax.ShapeDtypeStruct((t, 2 * MLSTM_HEADS), F32),
        jax.ShapeDtypeStruct((t, MLA_HEADS * QK_HEAD), BF16),
        jax.ShapeDtypeStruct((t, MLA_HEADS * QK_HEAD), BF16),
        jax.ShapeDtypeStruct((t, MLA_WIDTH), BF16),
    )
    return pl.pallas_call(
        _inproj_kernel,
        out_shape=out_shapes,
        grid=(t // tm,),
        in_specs=[row(D_MODEL), row(1), _const_spec(w_in_p.shape), _const_spec(q_norm.shape),
                  _const_spec(kv_norm.shape), _const_spec(wuq_p.shape), _const_spec(wukv_p.shape),
                  _const_spec(freq.shape), _const_spec(cmask.shape), _const_spec(smask.shape)],
        out_specs=tuple(row(s.shape[1]) for s in out_shapes),
        compiler_params=pltpu.CompilerParams(dimension_semantics=("parallel",),
                                             vmem_limit_bytes=VMEM_LIMIT_BYTES),
        name="inproj",
    )(x2, pos2, w_in_p, q_norm, kv_norm, wuq_p, wukv_p, freq, cmask, smask)


def _mlstm_kernel(q_ref, k_ref, v_ref, o_ref, gt_ref, brow_ref, nw_ref, sel_ref, tri_ref, y_ref, ct_ref, m_ref,
                  a8_ref, xt_ref):
    L = MLSTM_CHUNK
    d = HEAD_DIM
    nh = MLSTM_HEADS
    heads = range(nh)

    @pl.when(pl.program_id(1) == 0)
    def _():
        ct_ref[...] = jnp.zeros_like(ct_ref)
        m_ref[...] = jnp.zeros_like(m_ref)

    causal = lax.broadcasted_iota(jnp.int32, (L, L), 0) >= lax.broadcasted_iota(jnp.int32, (L, L), 1)
    top_rows = lax.broadcasted_iota(jnp.int32, (2 * nh, L), 0) < nh
    ones_blk = jnp.ones((L, d), BF16)
    neg_inf = jnp.float32(-jnp.inf)

    def hs(h):
        return slice(h * d, (h + 1) * d)

    def lanes(j):
        return slice(j * LANES, (j + 1) * LANES)

    def scalar_part(c, slot):
        g8 = gt_ref[0, c] + brow_ref[...]
        b8 = sum(jnp.dot(p, tri_ref[...], preferred_element_type=F32) for p in _split3(_log_sigmoid(g8)))
        a8 = g8 - pltpu.roll(b8, nh, 0)
        w8 = jnp.exp(a8 - jnp.max(a8, axis=-1, keepdims=True))
        r = jnp.where(top_rows, w8, b8)
        rpad = jnp.concatenate([p.astype(F32) for p in _split3(r)]
                               + [jnp.zeros((LANES - 3 * 2 * nh, L), F32)], axis=0)
        a8_ref[slot] = a8
        xt_ref[slot] = rpad.T.astype(BF16)

    def local_part(c, slot):
        rows = pl.ds(pl.multiple_of(c * L, L), L)
        a8 = a8_ref[slot]
        bc = jnp.dot(xt_ref[slot], sel_ref[...], preferred_element_type=F32)
        q = [q_ref[0, rows, hs(h)] for h in heads]
        k = [k_ref[0, rows, hs(h)] for h in heads]
        s = [lax.dot_general(q[h], k[h], (((1,), (1,)), ((), ())), preferred_element_type=F32) for h in heads]
        a_low = [jnp.where(causal, a8[h:h + 1, :], neg_inf) for h in heads]
        mp = [jnp.max(a_low[h], axis=-1, keepdims=True) for h in heads]
        a_mat = [(s[h] * jnp.exp(a_low[h] - mp[h])).astype(BF16) for h in heads]
        v_aug = [jnp.concatenate([v_ref[0, rows, hs(h)], ones_blk], axis=1) for h in heads]
        intra = [jnp.dot(a_mat[h], v_aug[h], preferred_element_type=F32) for h in heads]
        w_rep = [bc[:, lanes(h)] for h in heads]
        b_rep = [bc[:, lanes(nh + h)] for h in heads]
        wv = [jnp.concatenate([(v_aug[h][:, :d].astype(F32) * w_rep[h]).astype(BF16), w_rep[h].astype(BF16)], axis=1)
              for h in heads]
        upd = [lax.dot_general(k[h], wv[h], (((0,), (0,)), ((), ())), preferred_element_type=F32) for h in heads]
        return rows, q, mp, b_rep, intra, upd

    def carried_part(local):
        rows, q, mp, b_rep, intra, upd = local
        ct = [ct_ref[h] for h in heads]
        inter = [jnp.dot(q[h], ct[h].astype(BF16), preferred_element_type=F32) for h in heads]
        m_prev = [m_ref[h][0:1, :] for h in heads]
        for h in heads:
            mp_rep = jnp.broadcast_to(mp[h], (L, LANES))
            m_rep = jnp.maximum(mp_rep, m_prev[h])
            e_intra = jnp.exp(mp_rep - m_rep)
            e_inter = jnp.exp(m_prev[h] - m_rep)
            num = e_intra * intra[h][:, :d] + e_inter * inter[h][:, :d]
            den = e_intra * intra[h][:, d:] + e_inter * inter[h][:, d:]
            hraw = num / jnp.maximum(jnp.abs(den), jnp.exp(-(b_rep[h] + m_rep)))
            gate = jax.nn.sigmoid(o_ref[0, rows, hs(h)])
            y_ref[0, rows, hs(h)] = (_rms_norm(hraw, nw_ref[:, hs(h)]) * gate).astype(BF16)
            mp_last = mp_rep[L - 1:L, :]
            m_last = jnp.maximum(mp_last, m_prev[h])
            keep = jnp.exp(m_prev[h] - m_last)
            add = jnp.exp(mp_last - m_last)
            ct_ref[h] = (jnp.concatenate([keep, keep], axis=1) * ct[h]
                         + jnp.concatenate([add, add], axis=1) * upd[h])
            m_ref[h] = jnp.broadcast_to(b_rep[h][L - 1:L, :] + m_last, (SUBLANES, LANES))

    n_pairs = MLSTM_BLOCK // (2 * L)
    scalar_part(0, 0)
    scalar_part(1, 1)

    def pair(i, carry):
        first = local_part(2 * i, 0)
        second = local_part(2 * i + 1, 1)
        carried_part(first)
        carried_part(second)
        nxt = jnp.minimum(i + 1, n_pairs - 1)
        scalar_part(2 * nxt, 0)
        scalar_part(2 * nxt + 1, 1)
        return carry

    lax.fori_loop(0, n_pairs, pair, 0)


def _mlstm(qm, km, vm, om, gates_t, b_row, norm_w, sel, tri_t):
    b, s, _ = qm.shape
    tb = MLSTM_BLOCK
    blk = lambda width: pl.BlockSpec((1, tb, width), lambda bi, si: (bi, si, 0))
    return pl.pallas_call(
        _mlstm_kernel,
        out_shape=jax.ShapeDtypeStruct((b, s, MLSTM_WIDTH), BF16),
        grid=(b, s // tb),
        in_specs=[blk(MLSTM_WIDTH), blk(MLSTM_WIDTH), blk(MLSTM_WIDTH), blk(MLSTM_WIDTH),
                  pl.BlockSpec((1, tb // MLSTM_CHUNK, 2 * MLSTM_HEADS, MLSTM_CHUNK),
                               lambda bi, si: (bi, si, 0, 0)),
                  _const_spec(b_row.shape), _const_spec(norm_w.shape), _const_spec(sel.shape),
                  _const_spec(tri_t.shape)],
        out_specs=blk(MLSTM_WIDTH),
        scratch_shapes=[pltpu.VMEM((MLSTM_HEADS, HEAD_DIM, 2 * HEAD_DIM), F32),
                        pltpu.VMEM((MLSTM_HEADS, SUBLANES, LANES), F32),
                        pltpu.VMEM((2, 2 * MLSTM_HEADS, MLSTM_CHUNK), F32),
                        pltpu.VMEM((2, MLSTM_CHUNK, LANES), BF16)],
        compiler_params=pltpu.CompilerParams(dimension_semantics=("parallel", "arbitrary"),
                                             vmem_limit_bytes=VMEM_LIMIT_BYTES),
        name="mlstm",
    )(qm, km, vm, om, gates_t, b_row, norm_w, sel, tri_t)


def _mlstm_tri():
    idx = np.arange(MLSTM_CHUNK)
    return jnp.asarray(idx[:, None] <= idx[None, :], BF16)


def _mlstm_selector():
    r = np.arange(LANES)[:, None]
    c = np.arange(2 * MLSTM_HEADS * LANES)[None, :]
    return jnp.asarray((r < 3 * 2 * MLSTM_HEADS) & (r % (2 * MLSTM_HEADS) == c // LANES), BF16)


def _attn_kernel(q_ref, k_ref, v_ref, o_ref, s_ref, m_ref, acc_ref):
    t = ATTN_TILE
    nh = MLA_HEADS
    qi = pl.program_id(1)
    diag = (lax.broadcasted_iota(jnp.int32, (t, t), 0) >= lax.broadcasted_iota(jnp.int32, (t, t), 1))
    neg_inf = jnp.float32(-jnp.inf)
    ones_blk = jnp.ones((t, HEAD_DIM), BF16)
    m_ref[...] = jnp.full(m_ref.shape, neg_inf, F32)
    acc_ref[...] = jnp.zeros_like(acc_ref)

    def scores(j, slot, h):
        r0 = pl.multiple_of(j * t, t)
        q = q_ref[0, :, h * QK_HEAD:(h + 1) * QK_HEAD]
        k = k_ref[0, pl.ds(r0, t), h * QK_HEAD:(h + 1) * QK_HEAD]
        s_ref[slot * nh + h] = lax.dot_general(q, k, (((1,), (1,)), ((), ())),
                                               preferred_element_type=F32)

    def accumulate(j, slot, h, masked):
        r0 = pl.multiple_of(j * t, t)
        v_aug = jnp.concatenate(
            [v_ref[0, pl.ds(r0, t), h * HEAD_DIM:(h + 1) * HEAD_DIM], ones_blk], axis=1)
        s = s_ref[slot * nh + h]
        if masked:
            s = jnp.where(diag, s, neg_inf)
        m_old = m_ref[h]
        m_new = jnp.maximum(m_old, jnp.max(s, axis=-1, keepdims=True))
        p = jnp.exp2(s - jnp.concatenate([m_new] * (t // LANES), axis=1)).astype(BF16)
        alpha = jnp.exp2(m_old - m_new)
        acc_ref[h] = (jnp.concatenate([alpha, alpha], axis=1) * acc_ref[h]
                      + jnp.dot(p, v_aug, preferred_element_type=F32))
        m_ref[h] = m_new

    for h in range(nh):
        scores(0, 0, h)

    def half_step(j, slot):
        for h in range(nh):
            scores(j + 1, 1 - slot, h)
            accumulate(j, slot, h, masked=False)

    def body(i, carry):
        half_step(2 * i, 0)

        @pl.when(2 * i + 1 < qi)
        def _():
            half_step(2 * i + 1, 1)
        return carry

    lax.fori_loop(0, (qi + 1) // 2, body, 0)
    for h in range(nh):
        accumulate(qi, lax.rem(qi, 2), h, masked=True)
        acc = acc_ref[h]
        o_ref[0, :, h * HEAD_DIM:(h + 1) * HEAD_DIM] = (acc[:, :HEAD_DIM] / acc[:, HEAD_DIM:]).astype(BF16)


def _attention(qa, ka, va):
    b, s, _ = qa.shape
    t = ATTN_TILE
    return pl.pallas_call(
        _attn_kernel,
        out_shape=jax.ShapeDtypeStruct((b, s, MLA_WIDTH), BF16),
        grid=(b, s // t),
        in_specs=[pl.BlockSpec((1, t, MLA_HEADS * QK_HEAD), lambda bi, qi: (bi, qi, 0)),
                  pl.BlockSpec((1, s, MLA_HEADS * QK_HEAD), lambda bi, qi: (bi, 0, 0)),
                  pl.BlockSpec((1, s, MLA_WIDTH), lambda bi, qi: (bi, 0, 0))],
        out_specs=pl.BlockSpec((1, t, MLA_WIDTH), lambda bi, qi: (bi, qi, 0)),
        scratch_shapes=[pltpu.VMEM((2 * MLA_HEADS, t, t), F32),
                        pltpu.VMEM((MLA_HEADS, t, LANES), F32),
                        pltpu.VMEM((MLA_HEADS, t, 2 * HEAD_DIM), F32)],
        compiler_params=pltpu.CompilerParams(dimension_semantics=("parallel", "arbitrary"),
                                             vmem_limit_bytes=VMEM_LIMIT_BYTES),
        name="mla_attention",
    )(qa, ka, va)


def _outproj_kernel(ym_ref, ya_ref, x_ref, w_ref, g_ref, b_ref, o_ref):
    y = (jnp.dot(ym_ref[...], w_ref[0:MLSTM_WIDTH, :], preferred_element_type=F32)
         + jnp.dot(ya_ref[...], w_ref[MLSTM_WIDTH:, :], preferred_element_type=F32))
    o_ref[...] = _layer_norm(DEEPNORM_ALPHA * x_ref[...] + y, g_ref[...], b_ref[...])


def _outproj(ym2, ya2, x2, w_out, g, b):
    t = x2.shape[0]
    tm = ROW_TILE
    row = lambda width: pl.BlockSpec((tm, width), lambda i: (i, 0))
    return pl.pallas_call(
        _outproj_kernel,
        out_shape=jax.ShapeDtypeStruct((t, D_MODEL), F32),
        grid=(t // tm,),
        in_specs=[row(MLSTM_WIDTH), row(MLA_WIDTH), row(D_MODEL), _const_spec(w_out.shape),
                  _const_spec(g.shape), _const_spec(b.shape)],
        out_specs=row(D_MODEL),
        compiler_params=pltpu.CompilerParams(dimension_semantics=("parallel",),
                                             vmem_limit_bytes=VMEM_LIMIT_BYTES),
        name="outproj_ln",
    )(ym2, ya2, x2, w_out, g, b)


def _shift_rows(u, tail, k, row8):
    rolled = pltpu.roll(u, k, 0)
    top = jnp.where(row8 < k, pltpu.roll(tail, k, 0), rolled[0:SUBLANES])
    return jnp.concatenate([top, rolled[SUBLANES:]], axis=0)


def _ffn_kernel(x_ref, wup_ref, cw_ref, cb_ref, wdn_ref, g_ref, b_ref, o_ref, tail_ref, h_ref):
    tm = FFN_ROW_TILE
    tf = FFN_COLS

    @pl.when(pl.program_id(1) == 0)
    def _():
        tail_ref[...] = jnp.zeros_like(tail_ref)

    x = x_ref[0]
    xb = x.astype(BF16)
    row8 = lax.broadcasted_iota(jnp.int32, (SUBLANES, tf), 0)

    def conv_cols(lo):
        u = jnp.dot(xb, wup_ref[:, lo:lo + tf], preferred_element_type=F32)
        tail = tail_ref[:, lo:lo + tf]
        tail_ref[:, lo:lo + tf] = u[tm - SUBLANES:, :]
        return (_shift_rows(u, tail, 2, row8) * cw_ref[0:1, lo:lo + tf]
                + _shift_rows(u, tail, 1, row8) * cw_ref[1:2, lo:lo + tf]
                + u * cw_ref[2:3, lo:lo + tf] + cb_ref[:, lo:lo + tf])

    for c in range(FFN_DIM // tf):
        gate = conv_cols(c * tf)
        val = conv_cols(FFN_DIM + c * tf)
        h_ref[:, c * tf:(c + 1) * tf] = (gate * jax.nn.sigmoid(gate) * val).astype(BF16)

    y = jnp.dot(h_ref[...], wdn_ref[...], preferred_element_type=F32)
    o_ref[0] = _layer_norm(DEEPNORM_ALPHA * x + y, g_ref[...], b_ref[...])


def _ffn(x3, w_up, conv_w, conv_b, w_down, g, b):
    bsz, s, _ = x3.shape
    tm = FFN_ROW_TILE
    blk = pl.BlockSpec((1, tm, D_MODEL), lambda bi, si: (bi, si, 0))
    return pl.pallas_call(
        _ffn_kernel,
        out_shape=jax.ShapeDtypeStruct(x3.shape, F32),
        grid=(bsz, s // tm),
        in_specs=[blk, _const_spec(w_up.shape), _const_spec(conv_w.shape), _const_spec(conv_b.shape),
                  _const_spec(w_down.shape), _const_spec(g.shape), _const_spec(b.shape)],
        out_specs=blk,
        scratch_shapes=[pltpu.VMEM((SUBLANES, 2 * FFN_DIM), F32),
                        pltpu.VMEM((tm, FFN_DIM), BF16)],
        compiler_params=pltpu.CompilerParams(dimension_semantics=("parallel", "arbitrary"),
                                             vmem_limit_bytes=VMEM_LIMIT_BYTES),
        name="conv_ffn_ln",
    )(x3, w_up, conv_w, conv_b, w_down, g, b)


def _pool_kernel(x_ref, pw_ref, ls_ref, g_ref, b_ref, o_ref, xbuf_ref):
    tm = ROW_TILE
    si = pl.program_id(1)

    @pl.when(si == 0)
    def _():
        xbuf_ref[0:POOL_HALO, :] = jnp.zeros((POOL_HALO, D_MODEL), F32)

    x = x_ref[0]
    xbuf_ref[POOL_HALO:, :] = x
    t_pos = si * tm + lax.broadcasted_iota(jnp.int32, (tm, 1), 0)
    ys = []
    for gi, w in enumerate(POOL_WINDOWS):
        cols = slice(gi * POOL_GROUP_DIM, (gi + 1) * POOL_GROUP_DIM)
        acc = x[:, cols]
        for j in range(1, w):
            acc = acc + xbuf_ref[pl.ds(POOL_HALO - j, tm), cols]
        cnt = jnp.minimum(t_pos + 1, w).astype(F32)
        pooled = (acc / cnt - x[:, cols]).astype(BF16)
        ys.append(jnp.dot(pooled, pw_ref[gi], preferred_element_type=F32))
    y = jnp.concatenate(ys, axis=1) * ls_ref[...]
    o_ref[0] = _layer_norm(DEEPNORM_ALPHA * x + y, g_ref[...], b_ref[...])
    xbuf_ref[0:POOL_HALO, :] = x[tm - POOL_HALO:, :]


def _pool(x3, pool_w, layer_scale, g, b):
    bsz, s, _ = x3.shape
    tm = ROW_TILE
    blk = pl.BlockSpec((1, tm, D_MODEL), lambda bi, si: (bi, si, 0))
    return pl.pallas_call(
        _pool_kernel,
        out_shape=jax.ShapeDtypeStruct(x3.shape, F32),
        grid=(bsz, s // tm),
        in_specs=[blk, _const_spec(pool_w.shape), _const_spec(layer_scale.shape),
                  _const_spec(g.shape), _const_spec(b.shape)],
        out_specs=blk,
        scratch_shapes=[pltpu.VMEM((POOL_HALO + tm, D_MODEL), F32)],
        compiler_params=pltpu.CompilerParams(dimension_semantics=("parallel", "arbitrary"),
                                             vmem_limit_bytes=VMEM_LIMIT_BYTES),
        name="pool_ln",
    )(x3, pool_w, layer_scale, g, b)


def _rope_block_cols(w_rope):
    z = lambda n: jnp.zeros(w_rope.shape[:-1] + (n,), w_rope.dtype)
    return jnp.concatenate([z(ROPE_LO), w_rope[..., :ROPE_HALF], z(ROPE_HI - ROPE_LO - ROPE_HALF),
                            w_rope[..., ROPE_HALF:], z(LANES - ROPE_HI - ROPE_HALF)], axis=-1)


def _prep_w_in(w_in):
    offs = np.cumsum((0,) + IN_SIZES)
    parts = [w_in[:, offs[i]:offs[i + 1]] for i in range(len(IN_SIZES))]
    q_m, k_m, v_m, o_m, i_g, f_g, c_q, c_kv, k_r = parts
    gk = _rope_block_cols(k_r).at[:, 0:2 * MLSTM_HEADS].set(jnp.concatenate([i_g, f_g], axis=1))
    return jnp.concatenate([q_m, k_m, v_m, o_m, c_q, c_kv, gk], axis=1).astype(BF16)


def _prep_w_uq(w_uq):
    w = w_uq.reshape(Q_LORA, MLA_HEADS, HEAD_DIM + ROPE_DIM)
    blk = jnp.concatenate([w[..., :HEAD_DIM], _rope_block_cols(w[..., HEAD_DIM:])], axis=-1)
    return blk.reshape(Q_LORA, MLA_HEADS * QK_HEAD).astype(BF16)


def _prep_w_ukv(w_ukv):
    w = w_ukv.reshape(KV_LORA, MLA_HEADS, 2 * HEAD_DIM)
    k_nope = w[..., :HEAD_DIM].reshape(KV_LORA, MLA_WIDTH)
    v = w[..., HEAD_DIM:].reshape(KV_LORA, MLA_WIDTH)
    return jnp.concatenate([k_nope, v], axis=1).astype(BF16)


def _rope_lane_tables():
    inv_freq = ROPE_THETA ** (-jnp.arange(0, ROPE_DIM, 2, dtype=F32) / ROPE_DIM)
    freq = _rope_block_cols(jnp.concatenate([inv_freq, inv_freq])[None, :])
    ones = jnp.ones((1, ROPE_HALF), F32)
    cmask = _rope_block_cols(jnp.concatenate([ones, ones], axis=1))
    smask = _rope_block_cols(jnp.concatenate([-ones, ones], axis=1))
    return freq, cmask, smask


def kernel(x, positions, even_w_in, even_b_igate, even_b_fgate, even_mlstm_norm, even_q_norm,
           even_kv_norm, even_w_uq, even_w_ukv, even_w_out, odd_pool_w, odd_layer_scale,
           ffn_w_up, ffn_conv_w, ffn_conv_b, ffn_w_down, ln_mix_g, ln_mix_b, ln_ffn_g, ln_ffn_b):
    bsz, s, d = x.shape
    t = bsz * s
    row = lambda v: v.reshape(1, -1)

    freq, cmask, smask = _rope_lane_tables()
    qm, km, vm, om, gates, qa, ka, va = _inproj(
        x.reshape(t, d), positions.astype(F32).reshape(t, 1), _prep_w_in(even_w_in[0]),
        row(even_q_norm[0]), row(even_kv_norm[0]), _prep_w_uq(even_w_uq[0]), _prep_w_ukv(even_w_ukv[0]),
        freq, cmask, smask)

    gate_bias = jnp.concatenate([even_b_igate[0], even_b_fgate[0]])
    gates_t = gates.reshape(bsz, s // MLSTM_CHUNK, MLSTM_CHUNK, 2 * MLSTM_HEADS).transpose(0, 1, 3, 2)
    b3 = lambda a: a.reshape(bsz, s, a.shape[-1])
    ym = _mlstm(b3(qm), b3(km), b3(vm), b3(om), gates_t, gate_bias.reshape(-1, 1),
                row(even_mlstm_norm[0]), _mlstm_selector(), _mlstm_tri())
    ya = _attention(b3(qa), b3(ka), b3(va))

    x1 = _outproj(ym.reshape(t, MLSTM_WIDTH), ya.reshape(t, MLA_WIDTH), x.reshape(t, d),
                  even_w_out[0].astype(BF16), row(ln_mix_g[0]), row(ln_mix_b[0]))
    x1 = _ffn(x1.reshape(bsz, s, d), ffn_w_up[0].astype(BF16), ffn_conv_w[0], row(ffn_conv_b[0]),
              ffn_w_down[0].astype(BF16), row(ln_ffn_g[0]), row(ln_ffn_b[0]))

    x2 = _pool(x1, odd_pool_w[0].astype(BF16), row(odd_layer_scale[0]), row(ln_mix_g[1]), row(ln_mix_b[1]))
    return _ffn(x2, ffn_w_up[1].astype(BF16), ffn_conv_w[1], row(ffn_conv_b[1]),
                ffn_w_down[1].astype(BF16), row(ln_ffn_g[1]), row(ln_ffn_b[1]))
```

```python
import jax
import jax.numpy as jnp
import numpy as np
from jax import lax
from jax.experimental import pallas as pl
from jax.experimental.pallas import tpu as pltpu

F32 = jnp.float32
BF16 = jnp.bfloat16

D_MODEL = 1024
DEPTH = 2
MLSTM_HEADS = 4
HEAD_DIM = 128
MLSTM_WIDTH = MLSTM_HEADS * HEAD_DIM
MLA_HEADS = 4
ROPE_DIM = 64
ROPE_HALF = ROPE_DIM // 2
Q_LORA = 256
KV_LORA = 128
MLA_WIDTH = MLA_HEADS * HEAD_DIM
ROPE_THETA = 10000.0
POOL_WINDOWS = (2, 4, 8, 16)
POOL_GROUP_DIM = D_MODEL // len(POOL_WINDOWS)
FFN_DIM = 2816
CONV_WIDTH = 3
LN_EPS = 1e-5
RMS_EPS = 1e-6
DEEPNORM_ALPHA = (2 * DEPTH) ** 0.25
IN_SIZES = (MLSTM_WIDTH, MLSTM_WIDTH, MLSTM_WIDTH, MLSTM_WIDTH, MLSTM_HEADS, MLSTM_HEADS,
            Q_LORA, KV_LORA, ROPE_DIM)

LANES = 128
SUBLANES = 8
MXU_DIM = 256
VMEM_LIMIT_BYTES = 56 * 1024 * 1024

ROPE_LO = 8
ROPE_HI = ROPE_LO + LANES // 2
QK_HEAD = 2 * LANES

ROW_TILE = 512
MLSTM_CHUNK = 256
MLSTM_BLOCK = 1024
ATTN_TILE = 512
FFN_COLS = 256
FFN_ROW_TILE = 1024
FFN_SUB_ROWS = 256
OUTPROJ_ROW_TILE = 1024
POOL_ROW_TILE = 1024
POOL_HALO = 16


def _const_spec(shape):
    nd = len(shape)
    return pl.BlockSpec(shape, lambda *_: (0,) * nd, pipeline_mode=pl.Buffered(1))


def _layer_norm(z, g, b):
    mu = jnp.mean(z, axis=-1, keepdims=True)
    d = z - mu
    var = jnp.mean(d * d, axis=-1, keepdims=True)
    return d * lax.rsqrt(var + LN_EPS) * g + b


def _rms_norm(z, g):
    return z * lax.rsqrt(jnp.mean(z * z, axis=-1, keepdims=True) + RMS_EPS) * g


def _log_sigmoid(z):
    return -(jnp.maximum(-z, 0.0) + jnp.log1p(jnp.exp(-jnp.abs(z))))


def _split3(z):
    hi = z.astype(BF16)
    r1 = z - hi.astype(F32)
    mid = r1.astype(BF16)
    lo = (r1 - mid.astype(F32)).astype(BF16)
    return hi, mid, lo


def _inproj_kernel(x_ref, pos_ref, w_in_ref, qn_ref, kvn_ref, wuq_ref, wukv_ref, freq_ref, cmask_ref,
                   smask_ref, qm_ref, km_ref, vm_ref, om_ref, gates_ref, qa_ref, ka_ref, va_ref):
    ang = pos_ref[...] * freq_ref[...]
    cosm = jnp.cos(ang) * cmask_ref[...]
    sinm = jnp.sin(ang) * smask_ref[...]

    def rope(blk):
        return blk * cosm + pltpu.roll(blk, LANES // 2, 1) * sinm

    xb = x_ref[...].astype(BF16)

    def proj(lo, width):
        return jnp.dot(xb, w_in_ref[:, lo:lo + width], preferred_element_type=F32)

    w = MLSTM_WIDTH
    qm_ref[...] = proj(0, w).astype(BF16)
    km_ref[...] = (proj(w, w) * (HEAD_DIM ** -0.5)).astype(BF16)
    vm_ref[...] = proj(2 * w, w).astype(BF16)
    om_ref[...] = proj(3 * w, w)

    c_q = proj(4 * w, Q_LORA)
    ckv_gk = proj(4 * w + Q_LORA, KV_LORA + LANES)
    c_kv = ckv_gk[:, :KV_LORA]
    gk = ckv_gk[:, KV_LORA:]
    gates_ref[...] = gk[:, 0:2 * MLSTM_HEADS]
    k_rope = rope(gk).astype(BF16)

    scale = (HEAD_DIM + ROPE_DIM) ** -0.5 * float(np.log2(np.e))
    q = jnp.dot(_rms_norm(c_q, qn_ref[...]).astype(BF16), wuq_ref[...], preferred_element_type=F32)
    kv = jnp.dot(_rms_norm(c_kv, kvn_ref[...]).astype(BF16), wukv_ref[...], preferred_element_type=F32)
    for h in range(MLA_HEADS):
        lo = h * QK_HEAD
        qa_ref[:, lo:lo + LANES] = (q[:, lo:lo + LANES] * scale).astype(BF16)
        qa_ref[:, lo + LANES:lo + QK_HEAD] = (rope(q[:, lo + LANES:lo + QK_HEAD]) * scale).astype(BF16)
        ka_ref[:, lo:lo + LANES] = kv[:, h * HEAD_DIM:(h + 1) * HEAD_DIM].astype(BF16)
        ka_ref[:, lo + LANES:lo + QK_HEAD] = k_rope
    va_ref[...] = kv[:, MLA_WIDTH:].astype(BF16)


def _inproj(x2, pos2, w_in_p, q_norm, kv_norm, wuq_p, wukv_p, freq, cmask, smask):
    t = x2.shape[0]
    tm = ROW_TILE
    row = lambda width: pl.BlockSpec((tm, width), lambda i: (i, 0))
    out_shapes = (
        jax.ShapeDtypeStruct((t, MLSTM_WIDTH), BF16),
        jax.ShapeDtypeStruct((t, MLSTM_WIDTH), BF16),
        jax.ShapeDtypeStruct((t, MLSTM_WIDTH), BF16),
        jax.ShapeDtypeStruct((t, MLSTM_WIDTH), F32),
        jax.ShapeDtypeStruct((t, 2 * MLSTM_HEADS), F32),
        jax.ShapeDtypeStruct((t, MLA_HEADS * QK_HEAD), BF16),
        jax.ShapeDtypeStruct((t, MLA_HEADS * QK_HEAD), BF16),
        jax.ShapeDtypeStruct((t, MLA_WIDTH), BF16),
    )
    return pl.pallas_call(
        _inproj_kernel,
        out_shape=out_shapes,
        grid=(t // tm,),
        in_specs=[row(D_MODEL), row(1), _const_spec(w_in_p.shape), _const_spec(q_norm.shape),
                  _const_spec(kv_norm.shape), _const_spec(wuq_p.shape), _const_spec(wukv_p.shape),
                  _const_spec(freq.shape), _const_spec(cmask.shape), _const_spec(smask.shape)],
        out_specs=tuple(row(s.shape[1]) for s in out_shapes),
        compiler_params=pltpu.CompilerParams(dimension_semantics=("parallel",),
                                             vmem_limit_bytes=VMEM_LIMIT_BYTES),
        name="inproj",
    )(x2, pos2, w_in_p, q_norm, kv_norm, wuq_p, wukv_p, freq, cmask, smask)


def _mlstm_kernel(q_ref, k_ref, v_ref, o_ref, gt_ref, brow_ref, nw_ref, sel_ref, tri_ref, y_ref, ct_ref, m_ref,
                  a8_ref, xt_ref):
    L = MLSTM_CHUNK
    d = HEAD_DIM
    nh = MLSTM_HEADS
    heads = range(nh)

    @pl.when(pl.program_id(1) == 0)
    def _():
        ct_ref[...] = jnp.zeros_like(ct_ref)
        m_ref[...] = jnp.zeros_like(m_ref)

    causal = lax.broadcasted_iota(jnp.int32, (L, L), 0) >= lax.broadcasted_iota(jnp.int32, (L, L), 1)
    top_rows = lax.broadcasted_iota(jnp.int32, (2 * nh, L), 0) < nh
    ones_blk = jnp.ones((L, d), BF16)
    neg_inf = jnp.float32(-jnp.inf)

    def hs(h):
        return slice(h * d, (h + 1) * d)

    def lanes(j):
        return slice(j * LANES, (j + 1) * LANES)

    def scalar_part(c, slot):
        g8 = gt_ref[0, c] + brow_ref[...]
        b8 = sum(jnp.dot(p, tri_ref[...], preferred_element_type=F32) for p in _split3(_log_sigmoid(g8)))
        a8 = g8 - pltpu.roll(b8, nh, 0)
        w8 = jnp.exp(a8 - jnp.max(a8, axis=-1, keepdims=True))
        r = jnp.where(top_rows, w8, b8)
        rpad = jnp.concatenate([p.astype(F32) for p in _split3(r)]
                               + [jnp.zeros((LANES - 3 * 2 * nh, L), F32)], axis=0)
        a8_ref[slot] = a8
        xt_ref[slot] = rpad.T.astype(BF16)

    def local_part(c, slot):
        rows = pl.ds(pl.multiple_of(c * L, L), L)
        a8 = a8_ref[slot]
        bc = jnp.dot(xt_ref[slot], sel_ref[...], preferred_element_type=F32)
        q = [q_ref[0, rows, hs(h)] for h in heads]
        k = [k_ref[0, rows, hs(h)] for h in heads]
        s = [lax.dot_general(q[h], k[h], (((1,), (1,)), ((), ())), preferred_element_type=F32) for h in heads]
        a_low = [jnp.where(causal, a8[h:h + 1, :], neg_inf) for h in heads]
        mp = [jnp.max(a_low[h], axis=-1, keepdims=True) for h in heads]
        a_mat = [(s[h] * jnp.exp(a_low[h] - mp[h])).astype(BF16) for h in heads]
        v_aug = [jnp.concatenate([v_ref[0, rows, hs(h)], ones_blk], axis=1) for h in heads]
        intra = [jnp.dot(a_mat[h], v_aug[h], preferred_element_type=F32) for h in heads]
        w_rep = [bc[:, lanes(h)] for h in heads]
        b_rep = [bc[:, lanes(nh + h)] for h in heads]
        wv = [jnp.concatenate([(v_aug[h][:, :d].astype(F32) * w_rep[h]).astype(BF16), w_rep[h].astype(BF16)], axis=1)
              for h in heads]
        upd = [lax.dot_general(k[h], wv[h], (((0,), (0,)), ((), ())), preferred_element_type=F32) for h in heads]
        return rows, q, mp, b_rep, intra, upd

    def carried_part(local):
        rows, q, mp, b_rep, intra, upd = local
        ct = [ct_ref[h] for h in heads]
        inter = [jnp.dot(q[h], ct[h].astype(BF16), preferred_element_type=F32) for h in heads]
        m_prev = [m_ref[h][0:1, :] for h in heads]
        for h in heads:
            mp_rep = jnp.broadcast_to(mp[h], (L, LANES))
            m_rep = jnp.maximum(mp_rep, m_prev[h])
            e_intra = jnp.exp(mp_rep - m_rep)
            e_inter = jnp.exp(m_prev[h] - m_rep)
            num = e_intra * intra[h][:, :d] + e_inter * inter[h][:, :d]
            den = e_intra * intra[h][:, d:] + e_inter * inter[h][:, d:]
            hraw = num / jnp.maximum(jnp.abs(den), jnp.exp(-(b_rep[h] + m_rep)))
            gate = jax.nn.sigmoid(o_ref[0, rows, hs(h)])
            y_ref[0, rows, hs(h)] = (_rms_norm(hraw, nw_ref[:, hs(h)]) * gate).astype(BF16)
            mp_last = mp_rep[L - 1:L, :]
            m_last = jnp.maximum(mp_last, m_prev[h])
            keep = jnp.exp(m_prev[h] - m_last)
            add = jnp.exp(mp_last - m_last)
            ct_ref[h] = (jnp.concatenate([keep, keep], axis=1) * ct[h]
                         + jnp.concatenate([add, add], axis=1) * upd[h])
            m_ref[h] = jnp.broadcast_to(b_rep[h][L - 1:L, :] + m_last, (SUBLANES, LANES))

    n_pairs = MLSTM_BLOCK // (2 * L)
    scalar_part(0, 0)
    scalar_part(1, 1)

    def pair(i, carry):
        first = local_part(2 * i, 0)
        second = local_part(2 * i + 1, 1)
        carried_part(first)
        carried_part(second)
        nxt = jnp.minimum(i + 1, n_pairs - 1)
        scalar_part(2 * nxt, 0)
        scalar_part(2 * nxt + 1, 1)
        return carry

    lax.fori_loop(0, n_pairs, pair, 0)


def _mlstm(qm, km, vm, om, gates_t, b_row, norm_w, sel, tri_t):
    b, s, _ = qm.shape
    tb = MLSTM_BLOCK
    blk = lambda width: pl.BlockSpec((1, tb, width), lambda bi, si: (bi, si, 0))
    return pl.pallas_call(
        _mlstm_kernel,
        out_shape=jax.ShapeDtypeStruct((b, s, MLSTM_WIDTH), BF16),
        grid=(b, s // tb),
        in_specs=[blk(MLSTM_WIDTH), blk(MLSTM_WIDTH), blk(MLSTM_WIDTH), blk(MLSTM_WIDTH),
                  pl.BlockSpec((1, tb // MLSTM_CHUNK, 2 * MLSTM_HEADS, MLSTM_CHUNK),
                               lambda bi, si: (bi, si, 0, 0)),
                  _const_spec(b_row.shape), _const_spec(norm_w.shape), _const_spec(sel.shape),
                  _const_spec(tri_t.shape)],
        out_specs=blk(MLSTM_WIDTH),
        scratch_shapes=[pltpu.VMEM((MLSTM_HEADS, HEAD_DIM, 2 * HEAD_DIM), F32),
                        pltpu.VMEM((MLSTM_HEADS, SUBLANES, LANES), F32),
                        pltpu.VMEM((2, 2 * MLSTM_HEADS, MLSTM_CHUNK), F32),
                        pltpu.VMEM((2, MLSTM_CHUNK, LANES), BF16)],
        compiler_params=pltpu.CompilerParams(dimension_semantics=("parallel", "arbitrary"),
                                             vmem_limit_bytes=VMEM_LIMIT_BYTES),
        name="mlstm",
    )(qm, km, vm, om, gates_t, b_row, norm_w, sel, tri_t)


def _mlstm_tri():
    idx = np.arange(MLSTM_CHUNK)
    return jnp.asarray(idx[:, None] <= idx[None, :], BF16)


def _mlstm_selector():
    r = np.arange(LANES)[:, None]
    c = np.arange(2 * MLSTM_HEADS * LANES)[None, :]
    return jnp.asarray((r < 3 * 2 * MLSTM_HEADS) & (r % (2 * MLSTM_HEADS) == c // LANES), BF16)


def _attn_kernel(q_ref, k_ref, v_ref, o_ref, s_ref, m_ref, acc_ref):
    t = ATTN_TILE
    nh = MLA_HEADS
    qi = pl.program_id(1)
    diag = (lax.broadcasted_iota(jnp.int32, (t, t), 0) >= lax.broadcasted_iota(jnp.int32, (t, t), 1))
    neg_inf = jnp.float32(-jnp.inf)
    ones_blk = jnp.ones((t, HEAD_DIM), BF16)
    m_ref[...] = jnp.full(m_ref.shape, neg_inf, F32)
    acc_ref[...] = jnp.zeros_like(acc_ref)

    def scores(j, slot, h):
        r0 = pl.multiple_of(j * t, t)
        q = q_ref[0, :, h * QK_HEAD:(h + 1) * QK_HEAD]
        k = k_ref[0, pl.ds(r0, t), h * QK_HEAD:(h + 1) * QK_HEAD]
        s_ref[slot * nh + h] = lax.dot_general(q, k, (((1,), (1,)), ((), ())),
                                               preferred_element_type=F32)

    def accumulate(j, slot, h, masked):
        r0 = pl.multiple_of(j * t, t)
        v_aug = jnp.concatenate(
            [v_ref[0, pl.ds(r0, t), h * HEAD_DIM:(h + 1) * HEAD_DIM], ones_blk], axis=1)
        s = s_ref[slot * nh + h]
        if masked:
            s = jnp.where(diag, s, neg_inf)
        m_old = m_ref[h]
        m_new = jnp.maximum(m_old, jnp.max(s, axis=-1, keepdims=True))
        p = jnp.exp2(s - jnp.concatenate([m_new] * (t // LANES), axis=1)).astype(BF16)
        alpha = jnp.exp2(m_old - m_new)
        acc_ref[h] = (jnp.concatenate([alpha, alpha], axis=1) * acc_ref[h]
                      + jnp.dot(p, v_aug, preferred_element_type=F32))
        m_ref[h] = m_new

    for h in range(nh):
        scores(0, 0, h)

    def half_step(j, slot):
        for h in range(nh):
            scores(j + 1, 1 - slot, h)
            accumulate(j, slot, h, masked=False)

    def body(i, carry):
        half_step(2 * i, 0)

        @pl.when(2 * i + 1 < qi)
        def _():
            half_step(2 * i + 1, 1)
        return carry

    lax.fori_loop(0, (qi + 1) // 2, body, 0)
    for h in range(nh):
        accumulate(qi, lax.rem(qi, 2), h, masked=True)
        acc = acc_ref[h]
        o_ref[0, :, h * HEAD_DIM:(h + 1) * HEAD_DIM] = (acc[:, :HEAD_DIM] / acc[:, HEAD_DIM:]).astype(BF16)


def _attention(qa, ka, va):
    b, s, _ = qa.shape
    t = ATTN_TILE
    return pl.pallas_call(
        _attn_kernel,
        out_shape=jax.ShapeDtypeStruct((b, s, MLA_WIDTH), BF16),
        grid=(b, s // t),
        in_specs=[pl.BlockSpec((1, t, MLA_HEADS * QK_HEAD), lambda bi, qi: (bi, qi, 0)),
                  pl.BlockSpec((1, s, MLA_HEADS * QK_HEAD), lambda bi, qi: (bi, 0, 0)),
                  pl.BlockSpec((1, s, MLA_WIDTH), lambda bi, qi: (bi, 0, 0))],
        out_specs=pl.BlockSpec((1, t, MLA_WIDTH), lambda bi, qi: (bi, qi, 0)),
        scratch_shapes=[pltpu.VMEM((2 * MLA_HEADS, t, t), F32),
                        pltpu.VMEM((MLA_HEADS, t, LANES), F32),
                        pltpu.VMEM((MLA_HEADS, t, 2 * HEAD_DIM), F32)],
        compiler_params=pltpu.CompilerParams(dimension_semantics=("parallel", "arbitrary"),
                                             vmem_limit_bytes=VMEM_LIMIT_BYTES),
        name="mla_attention",
    )(qa, ka, va)


def _outproj_kernel(ym_ref, ya_ref, x_ref, w_ref, g_ref, b_ref, o_ref):
    tm = OUTPROJ_ROW_TILE
    n_split = 4
    sub = tm // n_split
    ys = []
    for r in range(n_split):
        rs = slice(r * sub, (r + 1) * sub)
        ys.append(jnp.dot(ym_ref[rs, :], w_ref[0:MLSTM_WIDTH, :], preferred_element_type=F32)
                  + jnp.dot(ya_ref[rs, :], w_ref[MLSTM_WIDTH:, :], preferred_element_type=F32))
    for r in range(n_split):
        rs = slice(r * sub, (r + 1) * sub)
        o_ref[rs, :] = _layer_norm(DEEPNORM_ALPHA * x_ref[rs, :] + ys[r], g_ref[...], b_ref[...])


def _outproj(ym2, ya2, x2, w_out, g, b):
    t = x2.shape[0]
    tm = OUTPROJ_ROW_TILE
    row = lambda width: pl.BlockSpec((tm, width), lambda i: (i, 0))
    return pl.pallas_call(
        _outproj_kernel,
        out_shape=jax.ShapeDtypeStruct((t, D_MODEL), F32),
        grid=(t // tm,),
        in_specs=[row(MLSTM_WIDTH), row(MLA_WIDTH), row(D_MODEL), _const_spec(w_out.shape),
                  _const_spec(g.shape), _const_spec(b.shape)],
        out_specs=row(D_MODEL),
        compiler_params=pltpu.CompilerParams(dimension_semantics=("parallel",),
                                             vmem_limit_bytes=VMEM_LIMIT_BYTES),
        name="outproj_ln",
    )(ym2, ya2, x2, w_out, g, b)


def _shift_rows(u, tail, k, row8):
    rolled = pltpu.roll(u, k, 0)
    top = jnp.where(row8 < k, pltpu.roll(tail, k, 0), rolled[0:SUBLANES])
    return jnp.concatenate([top, rolled[SUBLANES:]], axis=0)


def _ffn_kernel(x_ref, wup_ref, cw_ref, cb_ref, wdn_ref, g_ref, b_ref, o_ref, tail_ref, h_ref, xb_ref):
    tm = FFN_ROW_TILE
    tf = FFN_COLS
    sub = FFN_SUB_ROWS

    @pl.when(pl.program_id(1) == 0)
    def _():
        tail_ref[...] = jnp.zeros_like(tail_ref)

    xb_ref[...] = x_ref[0].astype(BF16)
    row8 = lax.broadcasted_iota(jnp.int32, (SUBLANES, tf), 0)

    def conv_cols(lo, r0, tail):
        u = jnp.dot(xb_ref[r0:r0 + sub, :], wup_ref[:, lo:lo + tf], preferred_element_type=F32)
        uc = (_shift_rows(u, tail, 2, row8) * cw_ref[0:1, lo:lo + tf]
              + _shift_rows(u, tail, 1, row8) * cw_ref[1:2, lo:lo + tf]
              + u * cw_ref[2:3, lo:lo + tf] + cb_ref[:, lo:lo + tf])
        return uc, u[sub - SUBLANES:, :]

    for c in range(FFN_DIM // tf):
        glo, vlo = c * tf, FFN_DIM + c * tf
        gtail = tail_ref[:, glo:glo + tf]
        vtail = tail_ref[:, vlo:vlo + tf]
        for r0 in range(0, tm, sub):
            gate, gtail = conv_cols(glo, r0, gtail)
            val, vtail = conv_cols(vlo, r0, vtail)
            h_ref[r0:r0 + sub, c * tf:(c + 1) * tf] = (gate * jax.nn.sigmoid(gate) * val).astype(BF16)
        tail_ref[:, glo:glo + tf] = gtail
        tail_ref[:, vlo:vlo + tf] = vtail

    y = jnp.dot(h_ref[...], wdn_ref[...], preferred_element_type=F32)
    o_ref[0] = _layer_norm(DEEPNORM_ALPHA * x_ref[0] + y, g_ref[...], b_ref[...])


def _ffn(x3, w_up, conv_w, conv_b, w_down, g, b):
    bsz, s, _ = x3.shape
    tm = FFN_ROW_TILE
    blk = pl.BlockSpec((1, tm, D_MODEL), lambda bi, si: (bi, si, 0))
    return pl.pallas_call(
        _ffn_kernel,
        out_shape=jax.ShapeDtypeStruct(x3.shape, F32),
        grid=(bsz, s // tm),
        in_specs=[blk, _const_spec(w_up.shape), _const_spec(conv_w.shape), _const_spec(conv_b.shape),
                  _const_spec(w_down.shape), _const_spec(g.shape), _const_spec(b.shape)],
        out_specs=blk,
        scratch_shapes=[pltpu.VMEM((SUBLANES, 2 * FFN_DIM), F32),
                        pltpu.VMEM((tm, FFN_DIM), BF16),
                        pltpu.VMEM((tm, D_MODEL), BF16)],
        compiler_params=pltpu.CompilerParams(dimension_semantics=("parallel", "arbitrary"),
                                             vmem_limit_bytes=VMEM_LIMIT_BYTES),
        name="conv_ffn_ln",
    )(x3, w_up, conv_w, conv_b, w_down, g, b)


def _pool_kernel(x_ref, pw_ref, ls_ref, g_ref, b_ref, o_ref, xbuf_ref):
    tm = POOL_ROW_TILE
    si = pl.program_id(1)

    @pl.when(si == 0)
    def _():
        xbuf_ref[0:POOL_HALO, :] = jnp.zeros((POOL_HALO, D_MODEL), F32)

    xbuf_ref[POOL_HALO:, :] = x_ref[0]
    n_split = 4
    sub = tm // n_split
    ys = []
    for r in range(n_split):
        r0 = r * sub
        t_pos = si * tm + r0 + lax.broadcasted_iota(jnp.int32, (sub, 1), 0)
        parts = []
        for gi, w in enumerate(POOL_WINDOWS):
            cols = slice(gi * POOL_GROUP_DIM, (gi + 1) * POOL_GROUP_DIM)
            ext = xbuf_ref[r0:r0 + POOL_HALO + sub, cols]
            k = 1
            while k < w:
                ext = ext + pltpu.roll(ext, k, 0)
                k *= 2
            cur = xbuf_ref[POOL_HALO + r0:POOL_HALO + r0 + sub, cols]
            cnt = jnp.minimum(t_pos + 1, w).astype(F32)
            pooled = (ext[POOL_HALO:, :] / cnt - cur).astype(BF16)
            parts.append(jnp.dot(pooled, pw_ref[gi], preferred_element_type=F32))
        ys.append(jnp.concatenate(parts, axis=1) * ls_ref[...])
    for r in range(n_split):
        rs = slice(r * sub, (r + 1) * sub)
        o_ref[0, rs, :] = _layer_norm(DEEPNORM_ALPHA * x_ref[0, rs, :] + ys[r], g_ref[...], b_ref[...])
    xbuf_ref[0:POOL_HALO, :] = x_ref[0, tm - POOL_HALO:, :]


def _pool(x3, pool_w, layer_scale, g, b):
    bsz, s, _ = x3.shape
    tm = POOL_ROW_TILE
    blk = pl.BlockSpec((1, tm, D_MODEL), lambda bi, si: (bi, si, 0))
    return pl.pallas_call(
        _pool_kernel,
        out_shape=jax.ShapeDtypeStruct(x3.shape, F32),
        grid=(bsz, s // tm),
        in_specs=[blk, _const_spec(pool_w.shape), _const_spec(layer_scale.shape),
                  _const_spec(g.shape), _const_spec(b.shape)],
        out_specs=blk,
        scratch_shapes=[pltpu.VMEM((POOL_HALO + tm, D_MODEL), F32)],
        compiler_params=pltpu.CompilerParams(dimension_semantics=("parallel", "arbitrary"),
                                             vmem_limit_bytes=VMEM_LIMIT_BYTES),
        name="pool_ln",
    )(x3, pool_w, layer_scale, g, b)


def _rope_block_cols(w_rope):
    z = lambda n: jnp.zeros(w_rope.shape[:-1] + (n,), w_rope.dtype)
    return jnp.concatenate([z(ROPE_LO), w_rope[..., :ROPE_HALF], z(ROPE_HI - ROPE_LO - ROPE_HALF),
                            w_rope[..., ROPE_HALF:], z(LANES - ROPE_HI - ROPE_HALF)], axis=-1)


def _prep_w_in(w_in):
    offs = np.cumsum((0,) + IN_SIZES)
    parts = [w_in[:, offs[i]:offs[i + 1]] for i in range(len(IN_SIZES))]
    q_m, k_m, v_m, o_m, i_g, f_g, c_q, c_kv, k_r = parts
    gk = _rope_block_cols(k_r).at[:, 0:2 * MLSTM_HEADS].set(jnp.concatenate([i_g, f_g], axis=1))
    return jnp.concatenate([q_m, k_m, v_m, o_m, c_q, c_kv, gk], axis=1).astype(BF16)


def _prep_w_uq(w_uq):
    w = w_uq.reshape(Q_LORA, MLA_HEADS, HEAD_DIM + ROPE_DIM)
    blk = jnp.concatenate([w[..., :HEAD_DIM], _rope_block_cols(w[..., HEAD_DIM:])], axis=-1)
    return blk.reshape(Q_LORA, MLA_HEADS * QK_HEAD).astype(BF16)


def _prep_w_ukv(w_ukv):
    w = w_ukv.reshape(KV_LORA, MLA_HEADS, 2 * HEAD_DIM)
    k_nope = w[..., :HEAD_DIM].reshape(KV_LORA, MLA_WIDTH)
    v = w[..., HEAD_DIM:].reshape(KV_LORA, MLA_WIDTH)
    return jnp.concatenate([k_nope, v], axis=1).astype(BF16)


def _rope_lane_tables():
    inv_freq = ROPE_THETA ** (-jnp.arange(0, ROPE_DIM, 2, dtype=F32) / ROPE_DIM)
    freq = _rope_block_cols(jnp.concatenate([inv_freq, inv_freq])[None, :])
    ones = jnp.ones((1, ROPE_HALF), F32)
    cmask = _rope_block_cols(jnp.concatenate([ones, ones], axis=1))
    smask = _rope_block_cols(jnp.concatenate([-ones, ones], axis=1))
    return freq, cmask, smask


def kernel(x, positions, even_w_in, even_b_igate, even_b_fgate, even_mlstm_norm, even_q_norm,
           even_kv_norm, even_w_uq, even_w_ukv, even_w_out, odd_pool_w, odd_layer_scale,
           ffn_w_up, ffn_conv_w, ffn_conv_b, ffn_w_down, ln_mix_g, ln_mix_b, ln_ffn_g, ln_ffn_b):
    bsz, s, d = x.shape
    t = bsz * s
    row = lambda v: v.reshape(1, -1)

    freq, cmask, smask = _rope_lane_tables()
    qm, km, vm, om, gates, qa, ka, va = _inproj(
        x.reshape(t, d), positions.astype(F32).reshape(t, 1), _prep_w_in(even_w_in[0]),
        row(even_q_norm[0]), row(even_kv_norm[0]), _prep_w_uq(even_w_uq[0]), _prep_w_ukv(even_w_ukv[0]),
        freq, cmask, smask)

    gate_bias = jnp.concatenate([even_b_igate[0], even_b_fgate[0]])
    gates_t = gates.reshape(bsz, s // MLSTM_CHUNK, MLSTM_CHUNK, 2 * MLSTM_HEADS).transpose(0, 1, 3, 2)
    b3 = lambda a: a.reshape(bsz, s, a.shape[-1])
    ym = _mlstm(b3(qm), b3(km), b3(vm), b3(om), gates_t, gate_bias.reshape(-1, 1),
                row(even_mlstm_norm[0]), _mlstm_selector(), _mlstm_tri())
    ya = _attention(b3(qa), b3(ka), b3(va))

    x1 = _outproj(ym.reshape(t, MLSTM_WIDTH), ya.reshape(t, MLA_WIDTH), x.reshape(t, d),
                  even_w_out[0].astype(BF16), row(ln_mix_g[0]), row(ln_mix_b[0]))
    x1 = _ffn(x1.reshape(bsz, s, d), ffn_w_up[0].astype(BF16), ffn_conv_w[0], row(ffn_conv_b[0]),
              ffn_w_down[0].astype(BF16), row(ln_ffn_g[0]), row(ln_ffn_b[0]))

    x2 = _pool(x1, odd_pool_w[0].astype(BF16), row(odd_layer_scale[0]), row(ln_mix_g[1]), row(ln_mix_b[1]))
    return _ffn(x2, ffn_w_up[1].astype(BF16), ffn_conv_w[1], row(ffn_conv_b[1]),
                ffn_w_down[1].astype(BF16), row(ln_ffn_g[1]), row(ln_ffn_b[1]))
```

```python
import jax
import jax.numpy as jnp
import numpy as np
from jax import lax
from jax.experimental import pallas as pl
from jax.experimental.pallas import tpu as pltpu

F32 = jnp.float32
BF16 = jnp.bfloat16

D_MODEL = 1024
DEPTH = 2
MLSTM_HEADS = 4
HEAD_DIM = 128
MLSTM_WIDTH = MLSTM_HEADS * HEAD_DIM
MLA_HEADS = 4
ROPE_DIM = 64
ROPE_HALF = ROPE_DIM // 2
Q_LORA = 256
KV_LORA = 128
MLA_WIDTH = MLA_HEADS * HEAD_DIM
ROPE_THETA = 10000.0
POOL_WINDOWS = (2, 4, 8, 16)
POOL_GROUP_DIM = D_MODEL // len(POOL_WINDOWS)
FFN_DIM = 2816
CONV_WIDTH = 3
LN_EPS = 1e-5
RMS_EPS = 1e-6
DEEPNORM_ALPHA = (2 * DEPTH) ** 0.25
IN_SIZES = (MLSTM_WIDTH, MLSTM_WIDTH, MLSTM_WIDTH, MLSTM_WIDTH, MLSTM_HEADS, MLSTM_HEADS,
            Q_LORA, KV_LORA, ROPE_DIM)

LANES = 128
SUBLANES = 8
MXU_DIM = 256
VMEM_LIMIT_BYTES = 56 * 1024 * 1024

ROPE_LO = 8
ROPE_HI = ROPE_LO + LANES // 2
QK_HEAD = 2 * LANES

ROW_TILE = 512
MLSTM_CHUNK = 256
MLSTM_BLOCK = 2048
ATTN_TILE = 512
FFN_COLS = 256
FFN_ROW_TILE = 1024
FFN_SUB_ROWS = 256
OUTPROJ_ROW_TILE = 1024
POOL_ROW_TILE = 1024
POOL_HALO = 16


def _const_spec(shape):
    nd = len(shape)
    return pl.BlockSpec(shape, lambda *_: (0,) * nd, pipeline_mode=pl.Buffered(1))


def _layer_norm(z, g, b):
    mu = jnp.mean(z, axis=-1, keepdims=True)
    d = z - mu
    var = jnp.mean(d * d, axis=-1, keepdims=True)
    return d * lax.rsqrt(var + LN_EPS) * g + b


def _rms_norm(z, g):
    return z * lax.rsqrt(jnp.mean(z * z, axis=-1, keepdims=True) + RMS_EPS) * g


def _log_sigmoid(z):
    return -(jnp.maximum(-z, 0.0) + jnp.log1p(jnp.exp(-jnp.abs(z))))


def _split3(z):
    hi = z.astype(BF16)
    r1 = z - hi.astype(F32)
    mid = r1.astype(BF16)
    lo = (r1 - mid.astype(F32)).astype(BF16)
    return hi, mid, lo


def _inproj_kernel(x_ref, pos_ref, w_in_ref, qn_ref, kvn_ref, wuq_ref, wukv_ref, freq_ref, cmask_ref,
                   smask_ref, qm_ref, km_ref, vm_ref, om_ref, gates_ref, qa_ref, ka_ref, va_ref):
    ang = pos_ref[...] * freq_ref[...]
    cosm = jnp.cos(ang) * cmask_ref[...]
    sinm = jnp.sin(ang) * smask_ref[...]

    def rope(blk):
        return blk * cosm + pltpu.roll(blk, LANES // 2, 1) * sinm

    xb = x_ref[...].astype(BF16)

    def proj(lo, width):
        return jnp.dot(xb, w_in_ref[:, lo:lo + width], preferred_element_type=F32)

    w = MLSTM_WIDTH
    qm_ref[...] = proj(0, w).astype(BF16)
    km_ref[...] = (proj(w, w) * (HEAD_DIM ** -0.5)).astype(BF16)
    vm_ref[...] = proj(2 * w, w).astype(BF16)
    om_ref[...] = proj(3 * w, w)

    c_q = proj(4 * w, Q_LORA)
    ckv_gk = proj(4 * w + Q_LORA, KV_LORA + LANES)
    c_kv = ckv_gk[:, :KV_LORA]
    gk = ckv_gk[:, KV_LORA:]
    gates_ref[...] = gk.T[0:2 * MLSTM_HEADS, :]
    k_rope = rope(gk).astype(BF16)

    scale = (HEAD_DIM + ROPE_DIM) ** -0.5 * float(np.log2(np.e))
    q = jnp.dot(_rms_norm(c_q, qn_ref[...]).astype(BF16), wuq_ref[...], preferred_element_type=F32)
    kv = jnp.dot(_rms_norm(c_kv, kvn_ref[...]).astype(BF16), wukv_ref[...], preferred_element_type=F32)
    for h in range(MLA_HEADS):
        lo = h * QK_HEAD
        qa_ref[:, lo:lo + LANES] = (q[:, lo:lo + LANES] * scale).astype(BF16)
        qa_ref[:, lo + LANES:lo + QK_HEAD] = (rope(q[:, lo + LANES:lo + QK_HEAD]) * scale).astype(BF16)
        ka_ref[:, lo:lo + LANES] = kv[:, h * HEAD_DIM:(h + 1) * HEAD_DIM].astype(BF16)
        ka_ref[:, lo + LANES:lo + QK_HEAD] = k_rope
    va_ref[...] = kv[:, MLA_WIDTH:].astype(BF16)


def _inproj(x2, pos2, w_in_p, q_norm, kv_norm, wuq_p, wukv_p, freq, cmask, smask):
    t = x2.shape[0]
    tm = ROW_TILE
    row = lambda width: pl.BlockSpec((tm, width), lambda i: (i, 0))
    out_shapes = (
        jax.ShapeDtypeStruct((t, MLSTM_WIDTH), BF16),
        jax.ShapeDtypeStruct((t, MLSTM_WIDTH), BF16),
        jax.ShapeDtypeStruct((t, MLSTM_WIDTH), BF16),
        jax.ShapeDtypeStruct((t, MLSTM_WIDTH), F32),
        jax.ShapeDtypeStruct((2 * MLSTM_HEADS, t), F32),
        jax.ShapeDtypeStruct((t, MLA_HEADS * QK_HEAD), BF16),
        jax.ShapeDtypeStruct((t, MLA_HEADS * QK_HEAD), BF16),
        jax.ShapeDtypeStruct((t, MLA_WIDTH), BF16),
    )
    return pl.pallas_call(
        _inproj_kernel,
        out_shape=out_shapes,
        grid=(t // tm,),
        in_specs=[row(D_MODEL), row(1), _const_spec(w_in_p.shape), _const_spec(q_norm.shape),
                  _const_spec(kv_norm.shape), _const_spec(wuq_p.shape), _const_spec(wukv_p.shape),
                  _const_spec(freq.shape), _const_spec(cmask.shape), _const_spec(smask.shape)],
        out_specs=tuple(pl.BlockSpec((2 * MLSTM_HEADS, tm), lambda i: (0, i)) if s.shape[0] != t else row(s.shape[1])
                        for s in out_shapes),
        compiler_params=pltpu.CompilerParams(dimension_semantics=("parallel",),
                                             vmem_limit_bytes=VMEM_LIMIT_BYTES),
        name="inproj",
    )(x2, pos2, w_in_p, q_norm, kv_norm, wuq_p, wukv_p, freq, cmask, smask)


def _mlstm_kernel(q_ref, k_ref, v_ref, o_ref, gt_ref, brow_ref, nw_ref, sel_ref, tri_ref, y_ref, ct_ref, m_ref,
                  a8_ref, xt_ref):
    L = MLSTM_CHUNK
    d = HEAD_DIM
    nh = MLSTM_HEADS
    heads = range(nh)

    @pl.when(pl.program_id(1) == 0)
    def _():
        ct_ref[...] = jnp.zeros_like(ct_ref)
        m_ref[...] = jnp.zeros_like(m_ref)

    causal = lax.broadcasted_iota(jnp.int32, (L, L), 0) >= lax.broadcasted_iota(jnp.int32, (L, L), 1)
    top_rows = lax.broadcasted_iota(jnp.int32, (2 * nh, L), 0) < nh
    ones_blk = jnp.ones((L, d), BF16)
    neg_inf = jnp.float32(-jnp.inf)

    def hs(h):
        return slice(h * d, (h + 1) * d)

    def lanes(j):
        return slice(j * LANES, (j + 1) * LANES)

    def scalar_part(c, slot):
        g8 = gt_ref[:, pl.ds(pl.multiple_of(c * L, L), L)] + brow_ref[...]
        b8 = sum(jnp.dot(p, tri_ref[...], preferred_element_type=F32) for p in _split3(_log_sigmoid(g8)))
        a8 = g8 - pltpu.roll(b8, nh, 0)
        w8 = jnp.exp(a8 - jnp.max(a8, axis=-1, keepdims=True))
        r = jnp.where(top_rows, w8, b8)
        rpad = jnp.concatenate([p.astype(F32) for p in _split3(r)]
                               + [jnp.zeros((LANES - 3 * 2 * nh, L), F32)], axis=0)
        a8_ref[slot] = a8
        xt_ref[slot] = rpad.T.astype(BF16)

    def local_part(c, slot):
        rows = pl.ds(pl.multiple_of(c * L, L), L)
        a8 = a8_ref[slot]
        bc = jnp.dot(xt_ref[slot], sel_ref[...], preferred_element_type=F32)
        q = [q_ref[0, rows, hs(h)] for h in heads]
        k = [k_ref[0, rows, hs(h)] for h in heads]
        s = [lax.dot_general(q[h], k[h], (((1,), (1,)), ((), ())), preferred_element_type=F32) for h in heads]
        a_low = [jnp.where(causal, a8[h:h + 1, :], neg_inf) for h in heads]
        mp = [jnp.max(a_low[h], axis=-1, keepdims=True) for h in heads]
        a_mat = [(s[h] * jnp.exp(a_low[h] - mp[h])).astype(BF16) for h in heads]
        v_aug = [jnp.concatenate([v_ref[0, rows, hs(h)], ones_blk], axis=1) for h in heads]
        intra = [jnp.dot(a_mat[h], v_aug[h], preferred_element_type=F32) for h in heads]
        w_rep = [bc[:, lanes(h)] for h in heads]
        b_rep = [bc[:, lanes(nh + h)] for h in heads]
        wv = [jnp.concatenate([(v_aug[h][:, :d].astype(F32) * w_rep[h]).astype(BF16), w_rep[h].astype(BF16)], axis=1)
              for h in heads]
        upd = [lax.dot_general(k[h], wv[h], (((0,), (0,)), ((), ())), preferred_element_type=F32) for h in heads]
        return rows, q, mp, b_rep, intra, upd

    def carried_part(local):
        rows, q, mp, b_rep, intra, upd = local
        ct = [ct_ref[h] for h in heads]
        inter = [jnp.dot(q[h], ct[h].astype(BF16), preferred_element_type=F32) for h in heads]
        m_prev = [m_ref[h][0:1, :] for h in heads]
        for h in heads:
            mp_rep = jnp.broadcast_to(mp[h], (L, LANES))
            m_rep = jnp.maximum(mp_rep, m_prev[h])
            e_intra = jnp.exp(mp_rep - m_rep)
            e_inter = jnp.exp(m_prev[h] - m_rep)
            num = e_intra * intra[h][:, :d] + e_inter * inter[h][:, :d]
            den = e_intra * intra[h][:, d:] + e_inter * inter[h][:, d:]
            hraw = num / jnp.maximum(jnp.abs(den), jnp.exp(-(b_rep[h] + m_rep)))
            gate = jax.nn.sigmoid(o_ref[0, rows, hs(h)])
            y_ref[0, rows, hs(h)] = (_rms_norm(hraw, nw_ref[:, hs(h)]) * gate).astype(BF16)
            mp_last = mp_rep[L - 1:L, :]
            m_last = jnp.maximum(mp_last, m_prev[h])
            keep = jnp.exp(m_prev[h] - m_last)
            add = jnp.exp(mp_last - m_last)
            ct_ref[h] = (jnp.concatenate([keep, keep], axis=1) * ct[h]
                         + jnp.concatenate([add, add], axis=1) * upd[h])
            m_ref[h] = jnp.broadcast_to(b_rep[h][L - 1:L, :] + m_last, (SUBLANES, LANES))

    n_pairs = MLSTM_BLOCK // (2 * L)
    scalar_part(0, 0)
    scalar_part(1, 1)

    def pair(i, carry):
        first = local_part(2 * i, 0)
        second = local_part(2 * i + 1, 1)
        carried_part(first)
        carried_part(second)
        nxt = jnp.minimum(i + 1, n_pairs - 1)
        scalar_part(2 * nxt, 0)
        scalar_part(2 * nxt + 1, 1)
        return carry

    lax.fori_loop(0, n_pairs, pair, 0)


def _mlstm(qm, km, vm, om, gates_t, b_row, norm_w, sel, tri_t):
    b, s, _ = qm.shape
    tb = MLSTM_BLOCK
    blk = lambda width: pl.BlockSpec((1, tb, width), lambda bi, si: (bi, si, 0))
    return pl.pallas_call(
        _mlstm_kernel,
        out_shape=jax.ShapeDtypeStruct((b, s, MLSTM_WIDTH), BF16),
        grid=(b, s // tb),
        in_specs=[blk(MLSTM_WIDTH), blk(MLSTM_WIDTH), blk(MLSTM_WIDTH), blk(MLSTM_WIDTH),
                  pl.BlockSpec((2 * MLSTM_HEADS, tb), lambda bi, si: (0, bi * (s // tb) + si)),
                  _const_spec(b_row.shape), _const_spec(norm_w.shape), _const_spec(sel.shape),
                  _const_spec(tri_t.shape)],
        out_specs=blk(MLSTM_WIDTH),
        scratch_shapes=[pltpu.VMEM((MLSTM_HEADS, HEAD_DIM, 2 * HEAD_DIM), F32),
                        pltpu.VMEM((MLSTM_HEADS, SUBLANES, LANES), F32),
                        pltpu.VMEM((2, 2 * MLSTM_HEADS, MLSTM_CHUNK), F32),
                        pltpu.VMEM((2, MLSTM_CHUNK, LANES), BF16)],
        compiler_params=pltpu.CompilerParams(dimension_semantics=("parallel", "arbitrary"),
                                             vmem_limit_bytes=VMEM_LIMIT_BYTES),
        name="mlstm",
    )(qm, km, vm, om, gates_t, b_row, norm_w, sel, tri_t)


def _mlstm_tri():
    idx = np.arange(MLSTM_CHUNK)
    return jnp.asarray(idx[:, None] <= idx[None, :], BF16)


def _mlstm_selector():
    r = np.arange(LANES)[:, None]
    c = np.arange(2 * MLSTM_HEADS * LANES)[None, :]
    return jnp.asarray((r < 3 * 2 * MLSTM_HEADS) & (r % (2 * MLSTM_HEADS) == c // LANES), BF16)


def _attn_kernel(q_ref, k_ref, v_ref, o_ref, s_ref, m_ref, acc_ref):
    t = ATTN_TILE
    nh = MLA_HEADS
    p = pl.program_id(1)
    half_tiles = pl.num_programs(1)
    diag = (lax.broadcasted_iota(jnp.int32, (t, t), 0) >= lax.broadcasted_iota(jnp.int32, (t, t), 1))
    neg_inf = jnp.float32(-jnp.inf)
    ones_blk = jnp.ones((t, HEAD_DIM), BF16)

    def reset():
        m_ref[...] = jnp.full(m_ref.shape, neg_inf, F32)
        acc_ref[...] = jnp.zeros_like(acc_ref)

    def scores(tile, j, slot, h):
        r0 = pl.multiple_of(j * t, t)
        q = q_ref[0, tile, :, h * QK_HEAD:(h + 1) * QK_HEAD]
        k = k_ref[0, pl.ds(r0, t), h * QK_HEAD:(h + 1) * QK_HEAD]
        s_ref[slot * nh + h] = lax.dot_general(q, k, (((1,), (1,)), ((), ())),
                                               preferred_element_type=F32)

    def accumulate(j, slot, h, masked):
        r0 = pl.multiple_of(j * t, t)
        v_aug = jnp.concatenate(
            [v_ref[0, pl.ds(r0, t), h * HEAD_DIM:(h + 1) * HEAD_DIM], ones_blk], axis=1)
        s = s_ref[slot * nh + h]
        if masked:
            s = jnp.where(diag, s, neg_inf)
        m_old = m_ref[h]
        m_new = jnp.maximum(m_old, jnp.max(s, axis=-1, keepdims=True))
        pr = jnp.exp2(s - jnp.concatenate([m_new] * (t // LANES), axis=1)).astype(BF16)
        alpha = jnp.exp2(m_old - m_new)
        acc_ref[h] = (jnp.concatenate([alpha, alpha], axis=1) * acc_ref[h]
                      + jnp.dot(pr, v_aug, preferred_element_type=F32))
        m_ref[h] = m_new

    def finish(tile, h):
        acc = acc_ref[h]
        o_ref[0, tile, :, h * HEAD_DIM:(h + 1) * HEAD_DIM] = (
            acc[:, :HEAD_DIM] / acc[:, HEAD_DIM:]).astype(BF16)

    def half_step(tile, j, slot_in, slot_out):
        for h in range(nh):
            scores(tile, j + 1, slot_out, h)
            accumulate(j, slot_in, h, masked=False)

    reset()
    for h in range(nh):
        scores(0, 0, 0, h)

    def body_a(i, carry):
        half_step(0, 2 * i, 0, 1)

        @pl.when(2 * i + 1 < p)
        def _():
            half_step(0, 2 * i + 1, 1, 0)
        return carry

    lax.fori_loop(0, (p + 1) // 2, body_a, 0)
    for parity in range(2):
        @pl.when(lax.rem(p, 2) == parity)
        def _(parity=parity):
            for h in range(nh):
                scores(1, 0, 2, h)
                accumulate(p, parity, h, masked=True)
                finish(0, h)

    reset()
    last = p + half_tiles
    half_step(1, 0, 2, 1)

    def body_b(i, carry):
        half_step(1, 2 * i + 1, 1, 0)

        @pl.when(2 * i + 2 < last)
        def _():
            half_step(1, 2 * i + 2, 0, 1)
        return carry

    lax.fori_loop(0, last // 2, body_b, 0)
    for h in range(nh):
        accumulate(last, lax.rem(last, 2), h, masked=True)
        finish(1, h)


def _attention(qa, ka, va):
    b, s, _ = qa.shape
    t = ATTN_TILE
    half = s // 2
    tile_pair = lambda width: pl.BlockSpec((1, 2, t, width), lambda bi, p: (bi, 0, p, 0))
    out = pl.pallas_call(
        _attn_kernel,
        out_shape=jax.ShapeDtypeStruct((b, 2, half, MLA_WIDTH), BF16),
        grid=(b, half // t),
        in_specs=[tile_pair(MLA_HEADS * QK_HEAD),
                  pl.BlockSpec((1, s, MLA_HEADS * QK_HEAD), lambda bi, p: (bi, 0, 0)),
                  pl.BlockSpec((1, s, MLA_WIDTH), lambda bi, p: (bi, 0, 0))],
        out_specs=tile_pair(MLA_WIDTH),
        scratch_shapes=[pltpu.VMEM((3 * MLA_HEADS, t, t), F32),
                        pltpu.VMEM((MLA_HEADS, t, LANES), F32),
                        pltpu.VMEM((MLA_HEADS, t, 2 * HEAD_DIM), F32)],
        compiler_params=pltpu.CompilerParams(dimension_semantics=("parallel", "arbitrary"),
                                             vmem_limit_bytes=VMEM_LIMIT_BYTES),
        name="mla_attention",
    )(qa.reshape(b, 2, half, MLA_HEADS * QK_HEAD), ka, va)
    return out.reshape(b, s, MLA_WIDTH)


def _outproj_kernel(ym_ref, ya_ref, x_ref, w_ref, g_ref, b_ref, o_ref):
    tm = OUTPROJ_ROW_TILE
    n_split = 4
    sub = tm // n_split
    ys = []
    for r in range(n_split):
        rs = slice(r * sub, (r + 1) * sub)
        ys.append(jnp.dot(ym_ref[rs, :], w_ref[0:MLSTM_WIDTH, :], preferred_element_type=F32)
                  + jnp.dot(ya_ref[rs, :], w_ref[MLSTM_WIDTH:, :], preferred_element_type=F32))
    for r in range(n_split):
        rs = slice(r * sub, (r + 1) * sub)
        o_ref[rs, :] = _layer_norm(DEEPNORM_ALPHA * x_ref[rs, :] + ys[r], g_ref[...], b_ref[...])


def _outproj(ym2, ya2, x2, w_out, g, b):
    t = x2.shape[0]
    tm = OUTPROJ_ROW_TILE
    row = lambda width: pl.BlockSpec((tm, width), lambda i: (i, 0))
    return pl.pallas_call(
        _outproj_kernel,
        out_shape=jax.ShapeDtypeStruct((t, D_MODEL), F32),
        grid=(t // tm,),
        in_specs=[row(MLSTM_WIDTH), row(MLA_WIDTH), row(D_MODEL), _const_spec(w_out.shape),
                  _const_spec(g.shape), _const_spec(b.shape)],
        out_specs=row(D_MODEL),
        compiler_params=pltpu.CompilerParams(dimension_semantics=("parallel",),
                                             vmem_limit_bytes=VMEM_LIMIT_BYTES),
        name="outproj_ln",
    )(ym2, ya2, x2, w_out, g, b)


def _shift_rows(u, tail, k, row8):
    rolled = pltpu.roll(u, k, 0)
    top = jnp.where(row8 < k, pltpu.roll(tail, k, 0), rolled[0:SUBLANES])
    return jnp.concatenate([top, rolled[SUBLANES:]], axis=0)


def _ffn_kernel(x_ref, wup_ref, cw_ref, cb_ref, wdn_ref, g_ref, b_ref, o_ref, tail_ref, h_ref, xb_ref):
    tm = FFN_ROW_TILE
    tf = FFN_COLS
    sub = FFN_SUB_ROWS

    @pl.when(pl.program_id(1) == 0)
    def _():
        tail_ref[...] = jnp.zeros_like(tail_ref)

    xb_ref[...] = x_ref[0].astype(BF16)
    row8 = lax.broadcasted_iota(jnp.int32, (SUBLANES, tf), 0)

    def conv_cols(lo, r0, tail):
        u = jnp.dot(xb_ref[r0:r0 + sub, :], wup_ref[:, lo:lo + tf], preferred_element_type=F32)
        uc = (_shift_rows(u, tail, 2, row8) * cw_ref[0:1, lo:lo + tf]
              + _shift_rows(u, tail, 1, row8) * cw_ref[1:2, lo:lo + tf]
              + u * cw_ref[2:3, lo:lo + tf] + cb_ref[:, lo:lo + tf])
        return uc, u[sub - SUBLANES:, :]

    for c in range(FFN_DIM // tf):
        glo, vlo = c * tf, FFN_DIM + c * tf
        gtail = tail_ref[:, glo:glo + tf]
        vtail = tail_ref[:, vlo:vlo + tf]
        for r0 in range(0, tm, sub):
            gate, gtail = conv_cols(glo, r0, gtail)
            val, vtail = conv_cols(vlo, r0, vtail)
            h_ref[r0:r0 + sub, c * tf:(c + 1) * tf] = (gate * jax.nn.sigmoid(gate) * val).astype(BF16)
        tail_ref[:, glo:glo + tf] = gtail
        tail_ref[:, vlo:vlo + tf] = vtail

    y = jnp.dot(h_ref[...], wdn_ref[...], preferred_element_type=F32)
    o_ref[0] = _layer_norm(DEEPNORM_ALPHA * x_ref[0] + y, g_ref[...], b_ref[...])


def _layer_spec(shape, layer):
    nd = len(shape) - 1
    return pl.BlockSpec((None,) + tuple(shape[1:]), lambda *_: (layer,) + (0,) * nd,
                        pipeline_mode=pl.Buffered(1))


def _ffn(x3, layer, w_up, conv_w, conv_b, w_down, g, b):
    bsz, s, _ = x3.shape
    tm = FFN_ROW_TILE
    blk = pl.BlockSpec((1, tm, D_MODEL), lambda bi, si: (bi, si, 0))
    return pl.pallas_call(
        _ffn_kernel,
        out_shape=jax.ShapeDtypeStruct(x3.shape, F32),
        grid=(bsz, s // tm),
        in_specs=[blk] + [_layer_spec(a.shape, layer) for a in (w_up, conv_w, conv_b, w_down, g, b)],
        out_specs=blk,
        scratch_shapes=[pltpu.VMEM((SUBLANES, 2 * FFN_DIM), F32),
                        pltpu.VMEM((tm, FFN_DIM), BF16),
                        pltpu.VMEM((tm, D_MODEL), BF16)],
        compiler_params=pltpu.CompilerParams(dimension_semantics=("parallel", "arbitrary"),
                                             vmem_limit_bytes=VMEM_LIMIT_BYTES),
        name="conv_ffn_ln",
    )(x3, w_up, conv_w, conv_b, w_down, g, b)


def _pool_kernel(x_ref, pw_ref, ls_ref, g_ref, b_ref, o_ref, xbuf_ref):
    tm = POOL_ROW_TILE
    si = pl.program_id(1)

    @pl.when(si == 0)
    def _():
        xbuf_ref[0:POOL_HALO, :] = jnp.zeros((POOL_HALO, D_MODEL), F32)

    xbuf_ref[POOL_HALO:, :] = x_ref[0]
    n_split = 4
    sub = tm // n_split
    ys = []
    for r in range(n_split):
        r0 = r * sub
        t_pos = si * tm + r0 + lax.broadcasted_iota(jnp.int32, (sub, 1), 0)
        parts = []
        for gi, w in enumerate(POOL_WINDOWS):
            cols = slice(gi * POOL_GROUP_DIM, (gi + 1) * POOL_GROUP_DIM)
            ext = xbuf_ref[r0:r0 + POOL_HALO + sub, cols]
            k = 1
            while k < w:
                ext = ext + pltpu.roll(ext, k, 0)
                k *= 2
            cur = xbuf_ref[POOL_HALO + r0:POOL_HALO + r0 + sub, cols]
            cnt = jnp.minimum(t_pos + 1, w).astype(F32)
            pooled = (ext[POOL_HALO:, :] / cnt - cur).astype(BF16)
            parts.append(jnp.dot(pooled, pw_ref[gi], preferred_element_type=F32))
        ys.append(jnp.concatenate(parts, axis=1) * ls_ref[...])
    for r in range(n_split):
        rs = slice(r * sub, (r + 1) * sub)
        o_ref[0, rs, :] = _layer_norm(DEEPNORM_ALPHA * x_ref[0, rs, :] + ys[r], g_ref[...], b_ref[...])
    xbuf_ref[0:POOL_HALO, :] = x_ref[0, tm - POOL_HALO:, :]


def _pool(x3, pool_w, layer_scale, g, b):
    bsz, s, _ = x3.shape
    tm = POOL_ROW_TILE
    blk = pl.BlockSpec((1, tm, D_MODEL), lambda bi, si: (bi, si, 0))
    return pl.pallas_call(
        _pool_kernel,
        out_shape=jax.ShapeDtypeStruct(x3.shape, F32),
        grid=(bsz, s // tm),
        in_specs=[blk, _const_spec(pool_w.shape), _const_spec(layer_scale.shape),
                  _const_spec(g.shape), _const_spec(b.shape)],
        out_specs=blk,
        scratch_shapes=[pltpu.VMEM((POOL_HALO + tm, D_MODEL), F32)],
        compiler_params=pltpu.CompilerParams(dimension_semantics=("parallel", "arbitrary"),
                                             vmem_limit_bytes=VMEM_LIMIT_BYTES),
        name="pool_ln",
    )(x3, pool_w, layer_scale, g, b)


def _rope_block_cols(w_rope):
    z = lambda n: jnp.zeros(w_rope.shape[:-1] + (n,), w_rope.dtype)
    return jnp.concatenate([z(ROPE_LO), w_rope[..., :ROPE_HALF], z(ROPE_HI - ROPE_LO - ROPE_HALF),
                            w_rope[..., ROPE_HALF:], z(LANES - ROPE_HI - ROPE_HALF)], axis=-1)


def _prep_w_in(w_in):
    offs = np.cumsum((0,) + IN_SIZES)
    parts = [w_in[:, offs[i]:offs[i + 1]] for i in range(len(IN_SIZES))]
    q_m, k_m, v_m, o_m, i_g, f_g, c_q, c_kv, k_r = parts
    gk = _rope_block_cols(k_r).at[:, 0:2 * MLSTM_HEADS].set(jnp.concatenate([i_g, f_g], axis=1))
    return jnp.concatenate([q_m, k_m, v_m, o_m, c_q, c_kv, gk], axis=1).astype(BF16)


def _prep_w_uq(w_uq):
    w = w_uq.reshape(Q_LORA, MLA_HEADS, HEAD_DIM + ROPE_DIM)
    blk = jnp.concatenate([w[..., :HEAD_DIM], _rope_block_cols(w[..., HEAD_DIM:])], axis=-1)
    return blk.reshape(Q_LORA, MLA_HEADS * QK_HEAD).astype(BF16)


def _prep_w_ukv(w_ukv):
    w = w_ukv.reshape(KV_LORA, MLA_HEADS, 2 * HEAD_DIM)
    k_nope = w[..., :HEAD_DIM].reshape(KV_LORA, MLA_WIDTH)
    v = w[..., HEAD_DIM:].reshape(KV_LORA, MLA_WIDTH)
    return jnp.concatenate([k_nope, v], axis=1).astype(BF16)


def _rope_lane_tables():
    inv_freq = ROPE_THETA ** (-jnp.arange(0, ROPE_DIM, 2, dtype=F32) / ROPE_DIM)
    freq = _rope_block_cols(jnp.concatenate([inv_freq, inv_freq])[None, :])
    ones = jnp.ones((1, ROPE_HALF), F32)
    cmask = _rope_block_cols(jnp.concatenate([ones, ones], axis=1))
    smask = _rope_block_cols(jnp.concatenate([-ones, ones], axis=1))
    return freq, cmask, smask


def kernel(x, positions, even_w_in, even_b_igate, even_b_fgate, even_mlstm_norm, even_q_norm,
           even_kv_norm, even_w_uq, even_w_ukv, even_w_out, odd_pool_w, odd_layer_scale,
           ffn_w_up, ffn_conv_w, ffn_conv_b, ffn_w_down, ln_mix_g, ln_mix_b, ln_ffn_g, ln_ffn_b):
    bsz, s, d = x.shape
    t = bsz * s
    row = lambda v: v.reshape(1, -1)

    freq, cmask, smask = _rope_lane_tables()
    qm, km, vm, om, gates, qa, ka, va = _inproj(
        x.reshape(t, d), positions.astype(F32).reshape(t, 1), _prep_w_in(even_w_in[0]),
        row(even_q_norm[0]), row(even_kv_norm[0]), _prep_w_uq(even_w_uq[0]), _prep_w_ukv(even_w_ukv[0]),
        freq, cmask, smask)

    gate_bias = jnp.concatenate([even_b_igate[0], even_b_fgate[0]])
    b3 = lambda a: a.reshape(bsz, s, a.shape[-1])
    ym = _mlstm(b3(qm), b3(km), b3(vm), b3(om), gates, gate_bias.reshape(-1, 1),
                row(even_mlstm_norm[0]), _mlstm_selector(), _mlstm_tri())
    ya = _attention(b3(qa), b3(ka), b3(va))

    x1 = _outproj(ym.reshape(t, MLSTM_WIDTH), ya.reshape(t, MLA_WIDTH), x.reshape(t, d),
                  even_w_out[0].astype(BF16), row(ln_mix_g[0]), row(ln_mix_b[0]))
    stack_rows = lambda v: v.reshape(DEPTH, 1, -1)
    ffn_params = (ffn_w_up.astype(BF16), ffn_conv_w, stack_rows(ffn_conv_b), ffn_w_down.astype(BF16),
                  stack_rows(ln_ffn_g), stack_rows(ln_ffn_b))
    x1 = _ffn(x1.reshape(bsz, s, d), 0, *ffn_params)

    x2 = _pool(x1, odd_pool_w[0].astype(BF16), row(odd_layer_scale[0]), row(ln_mix_g[1]), row(ln_mix_b[1]))
    return _ffn(x2, 1, *ffn_params)
```

```python
import jax
import jax.numpy as jnp
import numpy as np
from jax import lax
from jax.experimental import pallas as pl
from jax.experimental.pallas import tpu as pltpu

F32 = jnp.float32
BF16 = jnp.bfloat16

D_MODEL = 1024
DEPTH = 2
MLSTM_HEADS = 4
HEAD_DIM = 128
MLSTM_WIDTH = MLSTM_HEADS * HEAD_DIM
MLA_HEADS = 4
ROPE_DIM = 64
ROPE_HALF = ROPE_DIM // 2
Q_LORA = 256
KV_LORA = 128
MLA_WIDTH = MLA_HEADS * HEAD_DIM
ROPE_THETA = 10000.0
POOL_WINDOWS = (2, 4, 8, 16)
POOL_GROUP_DIM = D_MODEL // len(POOL_WINDOWS)
FFN_DIM = 2816
CONV_WIDTH = 3
LN_EPS = 1e-5
RMS_EPS = 1e-6
DEEPNORM_ALPHA = (2 * DEPTH) ** 0.25
IN_SIZES = (MLSTM_WIDTH, MLSTM_WIDTH, MLSTM_WIDTH, MLSTM_WIDTH, MLSTM_HEADS, MLSTM_HEADS,
            Q_LORA, KV_LORA, ROPE_DIM)

LANES = 128
SUBLANES = 8
MXU_DIM = 256
VMEM_LIMIT_BYTES = 56 * 1024 * 1024

ROPE_LO = 8
ROPE_HI = ROPE_LO + LANES // 2
QK_HEAD = 2 * LANES

ROW_TILE = 512
MLSTM_CHUNK = 256
MLSTM_BLOCK = 2048
ATTN_TILE = 512
FFN_COLS = 256
FFN_ROW_TILE = 1024
FFN_SUB_ROWS = 128
OUTPROJ_ROW_TILE = 1024
POOL_ROW_TILE = 1024
POOL_HALO = 16


def _const_spec(shape):
    nd = len(shape)
    return pl.BlockSpec(shape, lambda *_: (0,) * nd, pipeline_mode=pl.Buffered(1))


def _layer_norm(z, g, b):
    mu = jnp.mean(z, axis=-1, keepdims=True)
    d = z - mu
    var = jnp.mean(d * d, axis=-1, keepdims=True)
    return d * lax.rsqrt(var + LN_EPS) * g + b


def _rms_norm(z, g):
    return z * lax.rsqrt(jnp.mean(z * z, axis=-1, keepdims=True) + RMS_EPS) * g


def _log_sigmoid(z):
    return -(jnp.maximum(-z, 0.0) + jnp.log1p(jnp.exp(-jnp.abs(z))))


def _split3(z):
    hi = z.astype(BF16)
    r1 = z - hi.astype(F32)
    mid = r1.astype(BF16)
    lo = (r1 - mid.astype(F32)).astype(BF16)
    return hi, mid, lo


def _split2_lanes(z):
    hi = z.astype(BF16)
    return jnp.concatenate([hi, (z - hi.astype(F32)).astype(BF16)], axis=1)


def _inproj_kernel(x_ref, pos_ref, w_in_ref, qn_ref, kvn_ref, wuq_ref, wukv_ref, freq_ref, selc_ref,
                   sels_ref, qm_ref, km_ref, vm_ref, om_ref, gates_ref, qa_ref, ka_ref, va_ref):
    ang = pos_ref[...] * freq_ref[...]
    cos2 = _split2_lanes(jnp.cos(ang))
    sin2 = _split2_lanes(jnp.sin(ang))
    n_blk = ROW_TILE // LANES
    cosm = jnp.concatenate([jnp.dot(cos2, selc_ref[k], preferred_element_type=F32) for k in range(n_blk)], axis=0)
    sinm = jnp.concatenate([jnp.dot(sin2, sels_ref[k], preferred_element_type=F32) for k in range(n_blk)], axis=0)

    def rope(blk):
        return blk * cosm + pltpu.roll(blk, LANES // 2, 1) * sinm

    xb = x_ref[...].astype(BF16)

    def proj(lo, width):
        return jnp.dot(xb, w_in_ref[:, lo:lo + width], preferred_element_type=F32)

    w = MLSTM_WIDTH
    qm_ref[...] = proj(0, w).astype(BF16)
    km_ref[...] = (proj(w, w) * (HEAD_DIM ** -0.5)).astype(BF16)
    vm_ref[...] = proj(2 * w, w).astype(BF16)
    om_ref[...] = proj(3 * w, w)

    c_q = proj(4 * w, Q_LORA)
    ckv_gk = proj(4 * w + Q_LORA, KV_LORA + LANES)
    c_kv = ckv_gk[:, :KV_LORA]
    gk = ckv_gk[:, KV_LORA:]
    gates_ref[...] = gk.T[0:2 * MLSTM_HEADS, :]
    k_rope = rope(gk).astype(BF16)

    scale = (HEAD_DIM + ROPE_DIM) ** -0.5 * float(np.log2(np.e))
    q = jnp.dot(_rms_norm(c_q, qn_ref[...]).astype(BF16), wuq_ref[...], preferred_element_type=F32)
    kv = jnp.dot(_rms_norm(c_kv, kvn_ref[...]).astype(BF16), wukv_ref[...], preferred_element_type=F32)
    for h in range(MLA_HEADS):
        lo = h * QK_HEAD
        qa_ref[:, lo:lo + LANES] = (q[:, lo:lo + LANES] * scale).astype(BF16)
        qa_ref[:, lo + LANES:lo + QK_HEAD] = (rope(q[:, lo + LANES:lo + QK_HEAD]) * scale).astype(BF16)
        ka_ref[:, lo:lo + LANES] = kv[:, h * HEAD_DIM:(h + 1) * HEAD_DIM].astype(BF16)
        ka_ref[:, lo + LANES:lo + QK_HEAD] = k_rope
    va_ref[...] = kv[:, MLA_WIDTH:].astype(BF16)


def _inproj(x2, pos_packed, w_in_p, q_norm, kv_norm, wuq_p, wukv_p, freq, selc, sels):
    t = x2.shape[0]
    tm = ROW_TILE
    row = lambda width: pl.BlockSpec((tm, width), lambda i: (i, 0))
    out_shapes = (
        jax.ShapeDtypeStruct((t, MLSTM_WIDTH), BF16),
        jax.ShapeDtypeStruct((t, MLSTM_WIDTH), BF16),
        jax.ShapeDtypeStruct((t, MLSTM_WIDTH), BF16),
        jax.ShapeDtypeStruct((t, MLSTM_WIDTH), F32),
        jax.ShapeDtypeStruct((2 * MLSTM_HEADS, t), F32),
        jax.ShapeDtypeStruct((t, MLA_HEADS * QK_HEAD), BF16),
        jax.ShapeDtypeStruct((t, MLA_HEADS * QK_HEAD), BF16),
        jax.ShapeDtypeStruct((t, MLA_WIDTH), BF16),
    )
    return pl.pallas_call(
        _inproj_kernel,
        out_shape=out_shapes,
        grid=(t // tm,),
        in_specs=[row(D_MODEL), pl.BlockSpec((tm // 4, LANES), lambda i: (i, 0)),
                  _const_spec(w_in_p.shape), _const_spec(q_norm.shape),
                  _const_spec(kv_norm.shape), _const_spec(wuq_p.shape), _const_spec(wukv_p.shape),
                  _const_spec(freq.shape), _const_spec(selc.shape), _const_spec(sels.shape)],
        out_specs=tuple(pl.BlockSpec((2 * MLSTM_HEADS, tm), lambda i: (0, i)) if s.shape[0] != t else row(s.shape[1])
                        for s in out_shapes),
        compiler_params=pltpu.CompilerParams(dimension_semantics=("parallel",),
                                             vmem_limit_bytes=VMEM_LIMIT_BYTES),
        name="inproj",
    )(x2, pos_packed, w_in_p, q_norm, kv_norm, wuq_p, wukv_p, freq, selc, sels)


def _mlstm_kernel(q_ref, k_ref, v_ref, o_ref, gt_ref, brow_ref, nw_ref, sel_ref, tri_ref, y_ref, ct_ref, m_ref,
                  a8_ref, xt_ref):
    L = MLSTM_CHUNK
    d = HEAD_DIM
    nh = MLSTM_HEADS
    heads = range(nh)

    @pl.when(pl.program_id(1) == 0)
    def _():
        ct_ref[...] = jnp.zeros_like(ct_ref)
        m_ref[...] = jnp.zeros_like(m_ref)

    causal = lax.broadcasted_iota(jnp.int32, (L, L), 0) >= lax.broadcasted_iota(jnp.int32, (L, L), 1)
    top_rows = lax.broadcasted_iota(jnp.int32, (2 * nh, L), 0) < nh
    ones_blk = jnp.ones((L, d), BF16)
    neg_inf = jnp.float32(-jnp.inf)

    def hs(h):
        return slice(h * d, (h + 1) * d)

    def lanes(j):
        return slice(j * LANES, (j + 1) * LANES)

    def scalar_part(c, slot):
        g8 = gt_ref[:, pl.ds(pl.multiple_of(c * L, L), L)] + brow_ref[...]
        b8 = sum(jnp.dot(p, tri_ref[...], preferred_element_type=F32) for p in _split3(_log_sigmoid(g8)))
        a8 = g8 - pltpu.roll(b8, nh, 0)
        w8 = jnp.exp(a8 - jnp.max(a8, axis=-1, keepdims=True))
        r = jnp.where(top_rows, w8, b8)
        rpad = jnp.concatenate([p.astype(F32) for p in _split3(r)]
                               + [jnp.zeros((LANES - 3 * 2 * nh, L), F32)], axis=0)
        a8_ref[slot] = a8
        xt_ref[slot] = rpad.T.astype(BF16)

    def local_part(c, slot):
        rows = pl.ds(pl.multiple_of(c * L, L), L)
        a8 = a8_ref[slot]
        bc = jnp.dot(xt_ref[slot], sel_ref[...], preferred_element_type=F32)
        q = [q_ref[0, rows, hs(h)] for h in heads]
        k = [k_ref[0, rows, hs(h)] for h in heads]
        s = [lax.dot_general(q[h], k[h], (((1,), (1,)), ((), ())), preferred_element_type=F32) for h in heads]
        a_low = [jnp.where(causal, a8[h:h + 1, :], neg_inf) for h in heads]
        mp = [jnp.max(a_low[h], axis=-1, keepdims=True) for h in heads]
        a_mat = [(s[h] * jnp.exp(a_low[h] - mp[h])).astype(BF16) for h in heads]
        v_aug = [jnp.concatenate([v_ref[0, rows, hs(h)], ones_blk], axis=1) for h in heads]
        intra = [jnp.dot(a_mat[h], v_aug[h], preferred_element_type=F32) for h in heads]
        w_rep = [bc[:, lanes(h)] for h in heads]
        b_rep = [bc[:, lanes(nh + h)] for h in heads]
        wv = [jnp.concatenate([(v_aug[h][:, :d].astype(F32) * w_rep[h]).astype(BF16), w_rep[h].astype(BF16)], axis=1)
              for h in heads]
        upd = [lax.dot_general(k[h], wv[h], (((0,), (0,)), ((), ())), preferred_element_type=F32) for h in heads]
        return rows, q, mp, b_rep, intra, upd

    def carried_part(local):
        rows, q, mp, b_rep, intra, upd = local
        ct = [ct_ref[h] for h in heads]
        inter = [jnp.dot(q[h], ct[h].astype(BF16), preferred_element_type=F32) for h in heads]
        m_prev = [m_ref[h][0:1, :] for h in heads]
        for h in heads:
            mp_rep = jnp.broadcast_to(mp[h], (L, LANES))
            m_rep = jnp.maximum(mp_rep, m_prev[h])
            e_intra = jnp.exp(mp_rep - m_rep)
            e_inter = jnp.exp(m_prev[h] - m_rep)
            num = e_intra * intra[h][:, :d] + e_inter * inter[h][:, :d]
            den = e_intra * intra[h][:, d:] + e_inter * inter[h][:, d:]
            hraw = num / jnp.maximum(jnp.abs(den), jnp.exp(-(b_rep[h] + m_rep)))
            gate = jax.nn.sigmoid(o_ref[0, rows, hs(h)])
            y_ref[0, rows, hs(h)] = (_rms_norm(hraw, nw_ref[:, hs(h)]) * gate).astype(BF16)
            mp_last = mp_rep[L - 1:L, :]
            m_last = jnp.maximum(mp_last, m_prev[h])
            keep = jnp.exp(m_prev[h] - m_last)
            add = jnp.exp(mp_last - m_last)
            ct_ref[h] = (jnp.concatenate([keep, keep], axis=1) * ct[h]
                         + jnp.concatenate([add, add], axis=1) * upd[h])
            m_ref[h] = jnp.broadcast_to(b_rep[h][L - 1:L, :] + m_last, (SUBLANES, LANES))

    n_pairs = MLSTM_BLOCK // (2 * L)
    scalar_part(0, 0)
    scalar_part(1, 1)

    def pair(i, carry):
        first = local_part(2 * i, 0)
        second = local_part(2 * i + 1, 1)
        carried_part(first)
        carried_part(second)
        nxt = jnp.minimum(i + 1, n_pairs - 1)
        scalar_part(2 * nxt, 0)
        scalar_part(2 * nxt + 1, 1)
        return carry

    lax.fori_loop(0, n_pairs, pair, 0)


def _mlstm(qm, km, vm, om, gates_t, b_row, norm_w, sel, tri_t):
    b, s, _ = qm.shape
    tb = MLSTM_BLOCK
    blk = lambda width: pl.BlockSpec((1, tb, width), lambda bi, si: (bi, si, 0))
    return pl.pallas_call(
        _mlstm_kernel,
        out_shape=jax.ShapeDtypeStruct((b, s, MLSTM_WIDTH), BF16),
        grid=(b, s // tb),
        in_specs=[blk(MLSTM_WIDTH), blk(MLSTM_WIDTH), blk(MLSTM_WIDTH), blk(MLSTM_WIDTH),
                  pl.BlockSpec((2 * MLSTM_HEADS, tb), lambda bi, si: (0, bi * (s // tb) + si)),
                  _const_spec(b_row.shape), _const_spec(norm_w.shape), _const_spec(sel.shape),
                  _const_spec(tri_t.shape)],
        out_specs=blk(MLSTM_WIDTH),
        scratch_shapes=[pltpu.VMEM((MLSTM_HEADS, HEAD_DIM, 2 * HEAD_DIM), F32),
                        pltpu.VMEM((MLSTM_HEADS, SUBLANES, LANES), F32),
                        pltpu.VMEM((2, 2 * MLSTM_HEADS, MLSTM_CHUNK), F32),
                        pltpu.VMEM((2, MLSTM_CHUNK, LANES), BF16)],
        compiler_params=pltpu.CompilerParams(dimension_semantics=("parallel", "arbitrary"),
                                             vmem_limit_bytes=VMEM_LIMIT_BYTES),
        name="mlstm",
    )(qm, km, vm, om, gates_t, b_row, norm_w, sel, tri_t)


def _mlstm_tri():
    idx = np.arange(MLSTM_CHUNK)
    return jnp.asarray(idx[:, None] <= idx[None, :], BF16)


def _mlstm_selector():
    r = np.arange(LANES)[:, None]
    c = np.arange(2 * MLSTM_HEADS * LANES)[None, :]
    return jnp.asarray((r < 3 * 2 * MLSTM_HEADS) & (r % (2 * MLSTM_HEADS) == c // LANES), BF16)


def _attn_kernel(q_ref, k_ref, v_ref, o_ref, s_ref, m_ref, acc_ref):
    t = ATTN_TILE
    nh = MLA_HEADS
    p = pl.program_id(1)
    half_tiles = pl.num_programs(1)
    diag = (lax.broadcasted_iota(jnp.int32, (t, t), 0) >= lax.broadcasted_iota(jnp.int32, (t, t), 1))
    neg_inf = jnp.float32(-jnp.inf)
    ones_blk = jnp.ones((t, HEAD_DIM), BF16)

    def reset():
        m_ref[...] = jnp.full(m_ref.shape, neg_inf, F32)
        acc_ref[...] = jnp.zeros_like(acc_ref)

    def scores(tile, j, slot, h):
        r0 = pl.multiple_of(j * t, t)
        q = q_ref[0, tile, :, h * QK_HEAD:(h + 1) * QK_HEAD]
        k = k_ref[0, pl.ds(r0, t), h * QK_HEAD:(h + 1) * QK_HEAD]
        s_ref[slot * nh + h] = lax.dot_general(q, k, (((1,), (1,)), ((), ())),
                                               preferred_element_type=F32)

    def accumulate(j, slot, h, masked):
        r0 = pl.multiple_of(j * t, t)
        v_aug = jnp.concatenate(
            [v_ref[0, pl.ds(r0, t), h * HEAD_DIM:(h + 1) * HEAD_DIM], ones_blk], axis=1)
        s = s_ref[slot * nh + h]
        if masked:
            s = jnp.where(diag, s, neg_inf)
        m_old = m_ref[h]
        m_new = jnp.maximum(m_old, jnp.max(s, axis=-1, keepdims=True))
        pr = jnp.exp2(s - jnp.concatenate([m_new] * (t // LANES), axis=1)).astype(BF16)
        alpha = jnp.exp2(m_old - m_new)
        acc_ref[h] = (jnp.concatenate([alpha, alpha], axis=1) * acc_ref[h]
                      + jnp.dot(pr, v_aug, preferred_element_type=F32))
        m_ref[h] = m_new

    def finish(tile, h):
        acc = acc_ref[h]
        o_ref[0, tile, :, h * HEAD_DIM:(h + 1) * HEAD_DIM] = (
            acc[:, :HEAD_DIM] / acc[:, HEAD_DIM:]).astype(BF16)

    def half_step(tile, j, slot_in, slot_out):
        for h in range(nh):
            scores(tile, j + 1, slot_out, h)
            accumulate(j, slot_in, h, masked=False)

    reset()
    for h in range(nh):
        scores(0, 0, 0, h)

    def body_a(i, carry):
        half_step(0, 2 * i, 0, 1)

        @pl.when(2 * i + 1 < p)
        def _():
            half_step(0, 2 * i + 1, 1, 0)
        return carry

    lax.fori_loop(0, (p + 1) // 2, body_a, 0)
    for parity in range(2):
        @pl.when(lax.rem(p, 2) == parity)
        def _(parity=parity):
            for h in range(nh):
                scores(1, 0, 2, h)
                accumulate(p, parity, h, masked=True)
                finish(0, h)

    reset()
    last = p + half_tiles
    half_step(1, 0, 2, 1)

    def body_b(i, carry):
        half_step(1, 2 * i + 1, 1, 0)

        @pl.when(2 * i + 2 < last)
        def _():
            half_step(1, 2 * i + 2, 0, 1)
        return carry

    lax.fori_loop(0, last // 2, body_b, 0)
    for h in range(nh):
        accumulate(last, lax.rem(last, 2), h, masked=True)
        finish(1, h)


def _attention(qa, ka, va):
    b, s, _ = qa.shape
    t = ATTN_TILE
    half = s // 2
    tile_pair = lambda width: pl.BlockSpec((1, 2, t, width), lambda bi, p: (bi, 0, p, 0))
    out = pl.pallas_call(
        _attn_kernel,
        out_shape=jax.ShapeDtypeStruct((b, 2, half, MLA_WIDTH), BF16),
        grid=(b, half // t),
        in_specs=[tile_pair(MLA_HEADS * QK_HEAD),
                  pl.BlockSpec((1, s, MLA_HEADS * QK_HEAD), lambda bi, p: (bi, 0, 0)),
                  pl.BlockSpec((1, s, MLA_WIDTH), lambda bi, p: (bi, 0, 0))],
        out_specs=tile_pair(MLA_WIDTH),
        scratch_shapes=[pltpu.VMEM((3 * MLA_HEADS, t, t), F32),
                        pltpu.VMEM((MLA_HEADS, t, LANES), F32),
                        pltpu.VMEM((MLA_HEADS, t, 2 * HEAD_DIM), F32)],
        compiler_params=pltpu.CompilerParams(dimension_semantics=("parallel", "arbitrary"),
                                             vmem_limit_bytes=VMEM_LIMIT_BYTES),
        name="mla_attention",
    )(qa.reshape(b, 2, half, MLA_HEADS * QK_HEAD), ka, va)
    return out.reshape(b, s, MLA_WIDTH)


def _outproj_kernel(ym_ref, ya_ref, x_ref, w_ref, g_ref, b_ref, o_ref):
    tm = OUTPROJ_ROW_TILE
    n_split = 4
    sub = tm // n_split
    ys = []
    for r in range(n_split):
        rs = slice(r * sub, (r + 1) * sub)
        ys.append(jnp.dot(ym_ref[rs, :], w_ref[0:MLSTM_WIDTH, :], preferred_element_type=F32)
                  + jnp.dot(ya_ref[rs, :], w_ref[MLSTM_WIDTH:, :], preferred_element_type=F32))
    for r in range(n_split):
        rs = slice(r * sub, (r + 1) * sub)
        o_ref[rs, :] = _layer_norm(DEEPNORM_ALPHA * x_ref[rs, :] + ys[r], g_ref[...], b_ref[...])


def _outproj(ym2, ya2, x2, w_out, g, b):
    t = x2.shape[0]
    tm = OUTPROJ_ROW_TILE
    row = lambda width: pl.BlockSpec((tm, width), lambda i: (i, 0))
    return pl.pallas_call(
        _outproj_kernel,
        out_shape=jax.ShapeDtypeStruct((t, D_MODEL), F32),
        grid=(t // tm,),
        in_specs=[row(MLSTM_WIDTH), row(MLA_WIDTH), row(D_MODEL), _const_spec(w_out.shape),
                  _const_spec(g.shape), _const_spec(b.shape)],
        out_specs=row(D_MODEL),
        compiler_params=pltpu.CompilerParams(dimension_semantics=("parallel",),
                                             vmem_limit_bytes=VMEM_LIMIT_BYTES),
        name="outproj_ln",
    )(ym2, ya2, x2, w_out, g, b)


def _shift_rows(u, tail, k, row8):
    rolled = pltpu.roll(u, k, 0)
    top = jnp.where(row8 < k, pltpu.roll(tail, k, 0), rolled[0:SUBLANES])
    return jnp.concatenate([top, rolled[SUBLANES:]], axis=0)


def _ffn_kernel(x_ref, wup_ref, cw_ref, cb_ref, wdn_ref, g_ref, b_ref, o_ref, tail_ref, h_ref, xb_ref):
    tm = FFN_ROW_TILE
    tf = FFN_COLS
    sub = FFN_SUB_ROWS

    @pl.when(pl.program_id(1) == 0)
    def _():
        tail_ref[...] = jnp.zeros_like(tail_ref)

    xb_ref[...] = x_ref[0].astype(BF16)
    row8 = lax.broadcasted_iota(jnp.int32, (SUBLANES, tf), 0)

    def conv_cols(lo, r0, tail):
        u = jnp.dot(xb_ref[r0:r0 + sub, :], wup_ref[:, lo:lo + tf], preferred_element_type=F32)
        uc = (_shift_rows(u, tail, 2, row8) * cw_ref[0:1, lo:lo + tf]
              + _shift_rows(u, tail, 1, row8) * cw_ref[1:2, lo:lo + tf]
              + u * cw_ref[2:3, lo:lo + tf] + cb_ref[:, lo:lo + tf])
        return uc, u[sub - SUBLANES:, :]

    for c in range(FFN_DIM // tf):
        glo, vlo = c * tf, FFN_DIM + c * tf
        gtail = tail_ref[:, glo:glo + tf]
        vtail = tail_ref[:, vlo:vlo + tf]
        for r0 in range(0, tm, sub):
            gate, gtail = conv_cols(glo, r0, gtail)
            val, vtail = conv_cols(vlo, r0, vtail)
            h_ref[r0:r0 + sub, c * tf:(c + 1) * tf] = (gate * jax.nn.sigmoid(gate) * val).astype(BF16)
        tail_ref[:, glo:glo + tf] = gtail
        tail_ref[:, vlo:vlo + tf] = vtail

    y = jnp.dot(h_ref[...], wdn_ref[...], preferred_element_type=F32)
    o_ref[0] = _layer_norm(DEEPNORM_ALPHA * x_ref[0] + y, g_ref[...], b_ref[...])


def _layer_spec(shape, layer):
    nd = len(shape) - 1
    return pl.BlockSpec((None,) + tuple(shape[1:]), lambda *_: (layer,) + (0,) * nd,
                        pipeline_mode=pl.Buffered(1))


def _ffn(x3, layer, w_up, conv_w, conv_b, w_down, g, b):
    bsz, s, _ = x3.shape
    tm = FFN_ROW_TILE
    blk = pl.BlockSpec((1, tm, D_MODEL), lambda bi, si: (bi, si, 0))
    return pl.pallas_call(
        _ffn_kernel,
        out_shape=jax.ShapeDtypeStruct(x3.shape, F32),
        grid=(bsz, s // tm),
        in_specs=[blk] + [_layer_spec(a.shape, layer) for a in (w_up, conv_w, conv_b, w_down, g, b)],
        out_specs=blk,
        scratch_shapes=[pltpu.VMEM((SUBLANES, 2 * FFN_DIM), F32),
                        pltpu.VMEM((tm, FFN_DIM), BF16),
                        pltpu.VMEM((tm, D_MODEL), BF16)],
        compiler_params=pltpu.CompilerParams(dimension_semantics=("parallel", "arbitrary"),
                                             vmem_limit_bytes=VMEM_LIMIT_BYTES),
        name="conv_ffn_ln",
    )(x3, w_up, conv_w, conv_b, w_down, g, b)


def _pool_kernel(x_ref, pw_ref, ls_ref, g_ref, b_ref, o_ref, xbuf_ref):
    tm = POOL_ROW_TILE
    si = pl.program_id(1)

    @pl.when(si == 0)
    def _():
        xbuf_ref[0:POOL_HALO, :] = jnp.zeros((POOL_HALO, D_MODEL), F32)

    xbuf_ref[POOL_HALO:, :] = x_ref[0]
    n_split = 4
    sub = tm // n_split
    ys = []
    for r in range(n_split):
        r0 = r * sub
        t_pos = si * tm + r0 + lax.broadcasted_iota(jnp.int32, (sub, 1), 0)
        parts = []
        for gi, w in enumerate(POOL_WINDOWS):
            cols = slice(gi * POOL_GROUP_DIM, (gi + 1) * POOL_GROUP_DIM)
            ext = xbuf_ref[r0:r0 + POOL_HALO + sub, cols]
            k = 1
            while k < w:
                ext = ext + pltpu.roll(ext, k, 0)
                k *= 2
            cur = xbuf_ref[POOL_HALO + r0:POOL_HALO + r0 + sub, cols]
            cnt = jnp.minimum(t_pos + 1, w).astype(F32)
            pooled = (ext[POOL_HALO:, :] / cnt - cur).astype(BF16)
            parts.append(jnp.dot(pooled, pw_ref[gi], preferred_element_type=F32))
        ys.append(jnp.concatenate(parts, axis=1) * ls_ref[...])
    for r in range(n_split):
        rs = slice(r * sub, (r + 1) * sub)
        o_ref[0, rs, :] = _layer_norm(DEEPNORM_ALPHA * x_ref[0, rs, :] + ys[r], g_ref[...], b_ref[...])
    xbuf_ref[0:POOL_HALO, :] = x_ref[0, tm - POOL_HALO:, :]


def _pool(x3, pool_w, layer_scale, g, b):
    bsz, s, _ = x3.shape
    tm = POOL_ROW_TILE
    blk = pl.BlockSpec((1, tm, D_MODEL), lambda bi, si: (bi, si, 0))
    return pl.pallas_call(
        _pool_kernel,
        out_shape=jax.ShapeDtypeStruct(x3.shape, F32),
        grid=(bsz, s // tm),
        in_specs=[blk, _const_spec(pool_w.shape), _const_spec(layer_scale.shape),
                  _const_spec(g.shape), _const_spec(b.shape)],
        out_specs=blk,
        scratch_shapes=[pltpu.VMEM((POOL_HALO + tm, D_MODEL), F32)],
        compiler_params=pltpu.CompilerParams(dimension_semantics=("parallel", "arbitrary"),
                                             vmem_limit_bytes=VMEM_LIMIT_BYTES),
        name="pool_ln",
    )(x3, pool_w, layer_scale, g, b)


def _rope_block_cols(w_rope):
    z = lambda n: jnp.zeros(w_rope.shape[:-1] + (n,), w_rope.dtype)
    return jnp.concatenate([z(ROPE_LO), w_rope[..., :ROPE_HALF], z(ROPE_HI - ROPE_LO - ROPE_HALF),
                            w_rope[..., ROPE_HALF:], z(LANES - ROPE_HI - ROPE_HALF)], axis=-1)


def _prep_w_in(w_in):
    offs = np.cumsum((0,) + IN_SIZES)
    parts = [w_in[:, offs[i]:offs[i + 1]] for i in range(len(IN_SIZES))]
    q_m, k_m, v_m, o_m, i_g, f_g, c_q, c_kv, k_r = parts
    gk = _rope_block_cols(k_r).at[:, 0:2 * MLSTM_HEADS].set(jnp.concatenate([i_g, f_g], axis=1))
    return jnp.concatenate([q_m, k_m, v_m, o_m, c_q, c_kv, gk], axis=1).astype(BF16)


def _prep_w_uq(w_uq):
    w = w_uq.reshape(Q_LORA, MLA_HEADS, HEAD_DIM + ROPE_DIM)
    blk = jnp.concatenate([w[..., :HEAD_DIM], _rope_block_cols(w[..., HEAD_DIM:])], axis=-1)
    return blk.reshape(Q_LORA, MLA_HEADS * QK_HEAD).astype(BF16)


def _prep_w_ukv(w_ukv):
    w = w_ukv.reshape(KV_LORA, MLA_HEADS, 2 * HEAD_DIM)
    k_nope = w[..., :HEAD_DIM].reshape(KV_LORA, MLA_WIDTH)
    v = w[..., HEAD_DIM:].reshape(KV_LORA, MLA_WIDTH)
    return jnp.concatenate([k_nope, v], axis=1).astype(BF16)


def _rope_tables():
    inv_freq = ROPE_THETA ** (-jnp.arange(0, ROPE_DIM, 2, dtype=F32) / ROPE_DIM)
    n_blk = ROW_TILE // LANES
    freq = jnp.tile(inv_freq, n_blk)[None, :]
    selc = np.zeros((n_blk, 2 * LANES, LANES), np.float32)
    sels = np.zeros((n_blk, 2 * LANES, LANES), np.float32)
    f = np.arange(ROPE_HALF)
    for k in range(n_blk):
        for half in range(2):
            rows = half * LANES + ROPE_HALF * k + f
            selc[k, rows, ROPE_LO + f] = 1.0
            selc[k, rows, ROPE_HI + f] = 1.0
            sels[k, rows, ROPE_LO + f] = -1.0
            sels[k, rows, ROPE_HI + f] = 1.0
    return freq, jnp.asarray(selc, BF16), jnp.asarray(sels, BF16)


def _pack_positions(positions):
    n_blk = ROW_TILE // LANES
    pos = positions.astype(F32).reshape(-1, n_blk, LANES).transpose(0, 2, 1)
    return jnp.repeat(pos, LANES // n_blk, axis=-1).reshape(-1, LANES)


def kernel(x, positions, even_w_in, even_b_igate, even_b_fgate, even_mlstm_norm, even_q_norm,
           even_kv_norm, even_w_uq, even_w_ukv, even_w_out, odd_pool_w, odd_layer_scale,
           ffn_w_up, ffn_conv_w, ffn_conv_b, ffn_w_down, ln_mix_g, ln_mix_b, ln_ffn_g, ln_ffn_b):
    bsz, s, d = x.shape
    t = bsz * s
    row = lambda v: v.reshape(1, -1)

    freq, selc, sels = _rope_tables()
    qm, km, vm, om, gates, qa, ka, va = _inproj(
        x.reshape(t, d), _pack_positions(positions), _prep_w_in(even_w_in[0]),
        row(even_q_norm[0]), row(even_kv_norm[0]), _prep_w_uq(even_w_uq[0]), _prep_w_ukv(even_w_ukv[0]),
        freq, selc, sels)

    gate_bias = jnp.concatenate([even_b_igate[0], even_b_fgate[0]])
    b3 = lambda a: a.reshape(bsz, s, a.shape[-1])
    ym = _mlstm(b3(qm), b3(km), b3(vm), b3(om), gates, gate_bias.reshape(-1, 1),
                row(even_mlstm_norm[0]), _mlstm_selector(), _mlstm_tri())
    ya = _attention(b3(qa), b3(ka), b3(va))

    x1 = _outproj(ym.reshape(t, MLSTM_WIDTH), ya.reshape(t, MLA_WIDTH), x.reshape(t, d),
                  even_w_out[0].astype(BF16), row(ln_mix_g[0]), row(ln_mix_b[0]))
    stack_rows = lambda v: v.reshape(DEPTH, 1, -1)
    ffn_params = (ffn_w_up.astype(BF16), ffn_conv_w, stack_rows(ffn_conv_b), ffn_w_down.astype(BF16),
                  stack_rows(ln_ffn_g), stack_rows(ln_ffn_b))
    x1 = _ffn(x1.reshape(bsz, s, d), 0, *ffn_params)

    x2 = _pool(x1, odd_pool_w[0].astype(BF16), row(odd_layer_scale[0]), row(ln_mix_g[1]), row(ln_mix_b[1]))
    return _ffn(x2, 1, *ffn_params)
```

```python
import jax
import jax.numpy as jnp
import numpy as np
from jax import lax
from jax.experimental import pallas as pl
from jax.experimental.pallas import tpu as pltpu

F32 = jnp.float32
BF16 = jnp.bfloat16

D_MODEL = 1024
DEPTH = 2
MLSTM_HEADS = 4
HEAD_DIM = 128
MLSTM_WIDTH = MLSTM_HEADS * HEAD_DIM
MLA_HEADS = 4
ROPE_DIM = 64
ROPE_HALF = ROPE_DIM // 2
Q_LORA = 256
KV_LORA = 128
MLA_WIDTH = MLA_HEADS * HEAD_DIM
ROPE_THETA = 10000.0
POOL_WINDOWS = (2, 4, 8, 16)
POOL_GROUP_DIM = D_MODEL // len(POOL_WINDOWS)
FFN_DIM = 2816
CONV_WIDTH = 3
LN_EPS = 1e-5
RMS_EPS = 1e-6
DEEPNORM_ALPHA = (2 * DEPTH) ** 0.25
IN_SIZES = (MLSTM_WIDTH, MLSTM_WIDTH, MLSTM_WIDTH, MLSTM_WIDTH, MLSTM_HEADS, MLSTM_HEADS,
            Q_LORA, KV_LORA, ROPE_DIM)

LANES = 128
SUBLANES = 8
MXU_DIM = 256
VMEM_LIMIT_BYTES = 56 * 1024 * 1024

ROPE_LO = 8
ROPE_HI = ROPE_LO + LANES // 2
QK_HEAD = 2 * LANES

ROW_TILE = 512
MLSTM_CHUNK = 256
MLSTM_BLOCK = 2048
ATTN_TILE = 512
FFN_COLS = 256
FFN_ROW_TILE = 1024
FFN_SUB_ROWS = 128
POOL_ROW_TILE = 1024
POOL_HALO = 16


def _const_spec(shape):
    nd = len(shape)
    return pl.BlockSpec(shape, lambda *_: (0,) * nd, pipeline_mode=pl.Buffered(1))


def _layer_norm(z, g, b):
    mu = jnp.mean(z, axis=-1, keepdims=True)
    d = z - mu
    var = jnp.mean(d * d, axis=-1, keepdims=True)
    return d * lax.rsqrt(var + LN_EPS) * g + b


def _rms_norm(z, g):
    return z * lax.rsqrt(jnp.mean(z * z, axis=-1, keepdims=True) + RMS_EPS) * g


def _log_sigmoid(z):
    return -(jnp.maximum(-z, 0.0) + jnp.log1p(jnp.exp(-jnp.abs(z))))


def _split3(z):
    hi = z.astype(BF16)
    r1 = z - hi.astype(F32)
    mid = r1.astype(BF16)
    lo = (r1 - mid.astype(F32)).astype(BF16)
    return hi, mid, lo


def _split2_lanes(z):
    hi = z.astype(BF16)
    return jnp.concatenate([hi, (z - hi.astype(F32)).astype(BF16)], axis=1)


def _inproj_kernel(x_ref, pos_ref, w_in_ref, qn_ref, kvn_ref, wuq_ref, wukv_ref, freq_ref, selc_ref,
                   sels_ref, qm_ref, km_ref, vm_ref, om_ref, gates_ref, qa_ref, ka_ref, va_ref):
    ang = pos_ref[...] * freq_ref[...]
    cos2 = _split2_lanes(jnp.cos(ang))
    sin2 = _split2_lanes(jnp.sin(ang))
    n_blk = ROW_TILE // LANES
    cosm = jnp.concatenate([jnp.dot(cos2, selc_ref[k], preferred_element_type=F32) for k in range(n_blk)], axis=0)
    sinm = jnp.concatenate([jnp.dot(sin2, sels_ref[k], preferred_element_type=F32) for k in range(n_blk)], axis=0)

    def rope(blk):
        return blk * cosm + pltpu.roll(blk, LANES // 2, 1) * sinm

    xb = x_ref[...].astype(BF16)

    def proj(lo, width):
        return jnp.dot(xb, w_in_ref[:, lo:lo + width], preferred_element_type=F32)

    w = MLSTM_WIDTH
    qm_ref[...] = proj(0, w).astype(BF16)
    km_ref[...] = (proj(w, w) * (HEAD_DIM ** -0.5)).astype(BF16)
    vm_ref[...] = proj(2 * w, w).astype(BF16)
    om_ref[...] = proj(3 * w, w)

    c_q = proj(4 * w, Q_LORA)
    ckv_gk = proj(4 * w + Q_LORA, KV_LORA + LANES)
    c_kv = ckv_gk[:, :KV_LORA]
    gk = ckv_gk[:, KV_LORA:]
    gates_ref[...] = gk.T[0:2 * MLSTM_HEADS, :]
    k_rope = rope(gk).astype(BF16)

    scale = (HEAD_DIM + ROPE_DIM) ** -0.5 * float(np.log2(np.e))
    q = jnp.dot(_rms_norm(c_q, qn_ref[...]).astype(BF16), wuq_ref[...], preferred_element_type=F32)
    kv = jnp.dot(_rms_norm(c_kv, kvn_ref[...]).astype(BF16), wukv_ref[...], preferred_element_type=F32)
    for h in range(MLA_HEADS):
        lo = h * QK_HEAD
        qa_ref[:, lo:lo + LANES] = (q[:, lo:lo + LANES] * scale).astype(BF16)
        qa_ref[:, lo + LANES:lo + QK_HEAD] = (rope(q[:, lo + LANES:lo + QK_HEAD]) * scale).astype(BF16)
        ka_ref[:, lo:lo + LANES] = kv[:, h * HEAD_DIM:(h + 1) * HEAD_DIM].astype(BF16)
        ka_ref[:, lo + LANES:lo + QK_HEAD] = k_rope
    va_ref[...] = kv[:, MLA_WIDTH:].astype(BF16)


def _inproj(x2, pos_packed, w_in_p, q_norm, kv_norm, wuq_p, wukv_p, freq, selc, sels):
    t = x2.shape[0]
    tm = ROW_TILE
    row = lambda width: pl.BlockSpec((tm, width), lambda i: (i, 0))
    out_shapes = (
        jax.ShapeDtypeStruct((t, MLSTM_WIDTH), BF16),
        jax.ShapeDtypeStruct((t, MLSTM_WIDTH), BF16),
        jax.ShapeDtypeStruct((t, MLSTM_WIDTH), BF16),
        jax.ShapeDtypeStruct((t, MLSTM_WIDTH), F32),
        jax.ShapeDtypeStruct((2 * MLSTM_HEADS, t), F32),
        jax.ShapeDtypeStruct((t, MLA_HEADS * QK_HEAD), BF16),
        jax.ShapeDtypeStruct((t, MLA_HEADS * QK_HEAD), BF16),
        jax.ShapeDtypeStruct((t, MLA_WIDTH), BF16),
    )
    return pl.pallas_call(
        _inproj_kernel,
        out_shape=out_shapes,
        grid=(t // tm,),
        in_specs=[row(D_MODEL), pl.BlockSpec((tm // 4, LANES), lambda i: (i, 0)),
                  _const_spec(w_in_p.shape), _const_spec(q_norm.shape),
                  _const_spec(kv_norm.shape), _const_spec(wuq_p.shape), _const_spec(wukv_p.shape),
                  _const_spec(freq.shape), _const_spec(selc.shape), _const_spec(sels.shape)],
        out_specs=tuple(pl.BlockSpec((2 * MLSTM_HEADS, tm), lambda i: (0, i)) if s.shape[0] != t else row(s.shape[1])
                        for s in out_shapes),
        compiler_params=pltpu.CompilerParams(dimension_semantics=("parallel",),
                                             vmem_limit_bytes=VMEM_LIMIT_BYTES),
        name="inproj",
    )(x2, pos_packed, w_in_p, q_norm, kv_norm, wuq_p, wukv_p, freq, selc, sels)


def _mlstm_kernel(q_ref, k_ref, v_ref, o_ref, gt_ref, brow_ref, nw_ref, sel_ref, tri_ref, y_ref, ct_ref, m_ref,
                  a8_ref, xt_ref):
    L = MLSTM_CHUNK
    d = HEAD_DIM
    nh = MLSTM_HEADS
    heads = range(nh)

    @pl.when(pl.program_id(1) == 0)
    def _():
        ct_ref[...] = jnp.zeros_like(ct_ref)
        m_ref[...] = jnp.zeros_like(m_ref)

    causal = lax.broadcasted_iota(jnp.int32, (L, L), 0) >= lax.broadcasted_iota(jnp.int32, (L, L), 1)
    top_rows = lax.broadcasted_iota(jnp.int32, (2 * nh, L), 0) < nh
    ones_blk = jnp.ones((L, d), BF16)
    neg_inf = jnp.float32(-jnp.inf)

    def hs(h):
        return slice(h * d, (h + 1) * d)

    def lanes(j):
        return slice(j * LANES, (j + 1) * LANES)

    def scalar_part(c, slot):
        g8 = gt_ref[:, pl.ds(pl.multiple_of(c * L, L), L)] + brow_ref[...]
        b8 = sum(jnp.dot(p, tri_ref[...], preferred_element_type=F32) for p in _split3(_log_sigmoid(g8)))
        a8 = g8 - pltpu.roll(b8, nh, 0)
        w8 = jnp.exp(a8 - jnp.max(a8, axis=-1, keepdims=True))
        r = jnp.where(top_rows, w8, b8)
        rpad = jnp.concatenate([p.astype(F32) for p in _split3(r)]
                               + [jnp.zeros((LANES - 3 * 2 * nh, L), F32)], axis=0)
        a8_ref[slot] = a8
        xt_ref[slot] = rpad.T.astype(BF16)

    def local_part(c, slot):
        rows = pl.ds(pl.multiple_of(c * L, L), L)
        a8 = a8_ref[slot]
        bc = jnp.dot(xt_ref[slot], sel_ref[...], preferred_element_type=F32)
        q = [q_ref[0, rows, hs(h)] for h in heads]
        k = [k_ref[0, rows, hs(h)] for h in heads]
        s = [lax.dot_general(q[h], k[h], (((1,), (1,)), ((), ())), preferred_element_type=F32) for h in heads]
        a_low = [jnp.where(causal, a8[h:h + 1, :], neg_inf) for h in heads]
        mp = [jnp.max(a_low[h], axis=-1, keepdims=True) for h in heads]
        a_mat = [(s[h] * jnp.exp(a_low[h] - mp[h])).astype(BF16) for h in heads]
        v_aug = [jnp.concatenate([v_ref[0, rows, hs(h)], ones_blk], axis=1) for h in heads]
        intra = [jnp.dot(a_mat[h], v_aug[h], preferred_element_type=F32) for h in heads]
        w_rep = [bc[:, lanes(h)] for h in heads]
        b_rep = [bc[:, lanes(nh + h)] for h in heads]
        wv = [jnp.concatenate([(v_aug[h][:, :d].astype(F32) * w_rep[h]).astype(BF16), w_rep[h].astype(BF16)], axis=1)
              for h in heads]
        upd = [lax.dot_general(k[h], wv[h], (((0,), (0,)), ((), ())), preferred_element_type=F32) for h in heads]
        return rows, q, mp, b_rep, intra, upd

    def carried_part(local):
        rows, q, mp, b_rep, intra, upd = local
        ct = [ct_ref[h] for h in heads]
        inter = [jnp.dot(q[h], ct[h].astype(BF16), preferred_element_type=F32) for h in heads]
        m_prev = [m_ref[h][0:1, :] for h in heads]
        for h in heads:
            mp_rep = jnp.broadcast_to(mp[h], (L, LANES))
            m_rep = jnp.maximum(mp_rep, m_prev[h])
            e_intra = jnp.exp(mp_rep - m_rep)
            e_inter = jnp.exp(m_prev[h] - m_rep)
            num = e_intra * intra[h][:, :d] + e_inter * inter[h][:, :d]
            den = e_intra * intra[h][:, d:] + e_inter * inter[h][:, d:]
            hraw = num / jnp.maximum(jnp.abs(den), jnp.exp(-(b_rep[h] + m_rep)))
            gate = jax.nn.sigmoid(o_ref[0, rows, hs(h)])
            y_ref[0, rows, hs(h)] = (_rms_norm(hraw, nw_ref[:, hs(h)]) * gate).astype(BF16)
            mp_last = mp_rep[L - 1:L, :]
            m_last = jnp.maximum(mp_last, m_prev[h])
            keep = jnp.exp(m_prev[h] - m_last)
            add = jnp.exp(mp_last - m_last)
            ct_ref[h] = (jnp.concatenate([keep, keep], axis=1) * ct[h]
                         + jnp.concatenate([add, add], axis=1) * upd[h])
            m_ref[h] = jnp.broadcast_to(b_rep[h][L - 1:L, :] + m_last, (SUBLANES, LANES))

    n_pairs = MLSTM_BLOCK // (2 * L)
    scalar_part(0, 0)
    scalar_part(1, 1)

    def pair(i, carry):
        first = local_part(2 * i, 0)
        second = local_part(2 * i + 1, 1)
        carried_part(first)
        carried_part(second)
        nxt = jnp.minimum(i + 1, n_pairs - 1)
        scalar_part(2 * nxt, 0)
        scalar_part(2 * nxt + 1, 1)
        return carry

    lax.fori_loop(0, n_pairs, pair, 0)


def _mlstm(qm, km, vm, om, gates_t, b_row, norm_w, sel, tri_t):
    b, s, _ = qm.shape
    tb = MLSTM_BLOCK
    blk = lambda width: pl.BlockSpec((1, tb, width), lambda bi, si: (bi, si, 0))
    return pl.pallas_call(
        _mlstm_kernel,
        out_shape=jax.ShapeDtypeStruct((b, s, MLSTM_WIDTH), BF16),
        grid=(b, s // tb),
        in_specs=[blk(MLSTM_WIDTH), blk(MLSTM_WIDTH), blk(MLSTM_WIDTH), blk(MLSTM_WIDTH),
                  pl.BlockSpec((2 * MLSTM_HEADS, tb), lambda bi, si: (0, bi * (s // tb) + si)),
                  _const_spec(b_row.shape), _const_spec(norm_w.shape), _const_spec(sel.shape),
                  _const_spec(tri_t.shape)],
        out_specs=blk(MLSTM_WIDTH),
        scratch_shapes=[pltpu.VMEM((MLSTM_HEADS, HEAD_DIM, 2 * HEAD_DIM), F32),
                        pltpu.VMEM((MLSTM_HEADS, SUBLANES, LANES), F32),
                        pltpu.VMEM((2, 2 * MLSTM_HEADS, MLSTM_CHUNK), F32),
                        pltpu.VMEM((2, MLSTM_CHUNK, LANES), BF16)],
        compiler_params=pltpu.CompilerParams(dimension_semantics=("parallel", "arbitrary"),
                                             vmem_limit_bytes=VMEM_LIMIT_BYTES),
        name="mlstm",
    )(qm, km, vm, om, gates_t, b_row, norm_w, sel, tri_t)


def _mlstm_tri():
    idx = np.arange(MLSTM_CHUNK)
    return jnp.asarray(idx[:, None] <= idx[None, :], BF16)


def _mlstm_selector():
    r = np.arange(LANES)[:, None]
    c = np.arange(2 * MLSTM_HEADS * LANES)[None, :]
    return jnp.asarray((r < 3 * 2 * MLSTM_HEADS) & (r % (2 * MLSTM_HEADS) == c // LANES), BF16)


def _attn_kernel(q_ref, k_ref, v_ref, o_ref, s_ref, m_ref, acc_ref):
    t = ATTN_TILE
    nh = MLA_HEADS
    p = pl.program_id(1)
    half_tiles = pl.num_programs(1)
    diag = (lax.broadcasted_iota(jnp.int32, (t, t), 0) >= lax.broadcasted_iota(jnp.int32, (t, t), 1))
    neg_inf = jnp.float32(-jnp.inf)
    ones_blk = jnp.ones((t, HEAD_DIM), BF16)

    def reset():
        m_ref[...] = jnp.full(m_ref.shape, neg_inf, F32)
        acc_ref[...] = jnp.zeros_like(acc_ref)

    def scores(tile, j, slot, h):
        r0 = pl.multiple_of(j * t, t)
        q = q_ref[0, tile, :, h * QK_HEAD:(h + 1) * QK_HEAD]
        k = k_ref[0, pl.ds(r0, t), h * QK_HEAD:(h + 1) * QK_HEAD]
        s_ref[slot * nh + h] = lax.dot_general(q, k, (((1,), (1,)), ((), ())),
                                               preferred_element_type=F32)

    def accumulate(j, slot, h, masked):
        r0 = pl.multiple_of(j * t, t)
        v_aug = jnp.concatenate(
            [v_ref[0, pl.ds(r0, t), h * HEAD_DIM:(h + 1) * HEAD_DIM], ones_blk], axis=1)
        s = s_ref[slot * nh + h]
        if masked:
            s = jnp.where(diag, s, neg_inf)
        m_old = m_ref[h]
        m_new = jnp.maximum(m_old, jnp.max(s, axis=-1, keepdims=True))
        pr = jnp.exp2(s - jnp.concatenate([m_new] * (t // LANES), axis=1)).astype(BF16)
        alpha = jnp.exp2(m_old - m_new)
        acc_ref[h] = (jnp.concatenate([alpha, alpha], axis=1) * acc_ref[h]
                      + jnp.dot(pr, v_aug, preferred_element_type=F32))
        m_ref[h] = m_new

    def finish(tile, h):
        acc = acc_ref[h]
        o_ref[0, tile, :, h * HEAD_DIM:(h + 1) * HEAD_DIM] = (
            acc[:, :HEAD_DIM] / acc[:, HEAD_DIM:]).astype(BF16)

    def half_step(tile, j, slot_in, slot_out):
        for h in range(nh):
            scores(tile, j + 1, slot_out, h)
            accumulate(j, slot_in, h, masked=False)

    reset()
    for h in range(nh):
        scores(0, 0, 0, h)

    def body_a(i, carry):
        half_step(0, 2 * i, 0, 1)

        @pl.when(2 * i + 1 < p)
        def _():
            half_step(0, 2 * i + 1, 1, 0)
        return carry

    lax.fori_loop(0, (p + 1) // 2, body_a, 0)
    for parity in range(2):
        @pl.when(lax.rem(p, 2) == parity)
        def _(parity=parity):
            for h in range(nh):
                scores(1, 0, 2, h)
                accumulate(p, parity, h, masked=True)
                finish(0, h)

    reset()
    last = p + half_tiles
    half_step(1, 0, 2, 1)

    def body_b(i, carry):
        half_step(1, 2 * i + 1, 1, 0)

        @pl.when(2 * i + 2 < last)
        def _():
            half_step(1, 2 * i + 2, 0, 1)
        return carry

    lax.fori_loop(0, last // 2, body_b, 0)
    for h in range(nh):
        accumulate(last, lax.rem(last, 2), h, masked=True)
        finish(1, h)


def _attention(qa, ka, va):
    b, s, _ = qa.shape
    t = ATTN_TILE
    half = s // 2
    tile_pair = lambda width: pl.BlockSpec((1, 2, t, width), lambda bi, p: (bi, 0, p, 0))
    out = pl.pallas_call(
        _attn_kernel,
        out_shape=jax.ShapeDtypeStruct((b, 2, half, MLA_WIDTH), BF16),
        grid=(b, half // t),
        in_specs=[tile_pair(MLA_HEADS * QK_HEAD),
                  pl.BlockSpec((1, s, MLA_HEADS * QK_HEAD), lambda bi, p: (bi, 0, 0)),
                  pl.BlockSpec((1, s, MLA_WIDTH), lambda bi, p: (bi, 0, 0))],
        out_specs=tile_pair(MLA_WIDTH),
        scratch_shapes=[pltpu.VMEM((3 * MLA_HEADS, t, t), F32),
                        pltpu.VMEM((MLA_HEADS, t, LANES), F32),
                        pltpu.VMEM((MLA_HEADS, t, 2 * HEAD_DIM), F32)],
        compiler_params=pltpu.CompilerParams(dimension_semantics=("parallel", "arbitrary"),
                                             vmem_limit_bytes=VMEM_LIMIT_BYTES),
        name="mla_attention",
    )(qa.reshape(b, 2, half, MLA_HEADS * QK_HEAD), ka, va)
    return out.reshape(b, s, MLA_WIDTH)


def _shift_rows(u, tail, k, row8):
    rolled = pltpu.roll(u, k, 0)
    top = jnp.where(row8 < k, pltpu.roll(tail, k, 0), rolled[0:SUBLANES])
    return jnp.concatenate([top, rolled[SUBLANES:]], axis=0)


def _mlp_tile(res_ref, xb_ref, wup_ref, cw_ref, cb_ref, wdn_ref, g_ref, b_ref, o_ref, tail_ref, h_ref):
    tm = FFN_ROW_TILE
    tf = FFN_COLS
    sub = FFN_SUB_ROWS
    row8 = lax.broadcasted_iota(jnp.int32, (SUBLANES, tf), 0)

    def conv_cols(lo, r0, tail):
        u = jnp.dot(xb_ref[r0:r0 + sub, :], wup_ref[:, lo:lo + tf], preferred_element_type=F32)
        uc = (_shift_rows(u, tail, 2, row8) * cw_ref[0:1, lo:lo + tf]
              + _shift_rows(u, tail, 1, row8) * cw_ref[1:2, lo:lo + tf]
              + u * cw_ref[2:3, lo:lo + tf] + cb_ref[:, lo:lo + tf])
        return uc, u[sub - SUBLANES:, :]

    for c in range(FFN_DIM // tf):
        glo, vlo = c * tf, FFN_DIM + c * tf
        gtail = tail_ref[:, glo:glo + tf]
        vtail = tail_ref[:, vlo:vlo + tf]
        for r0 in range(0, tm, sub):
            gate, gtail = conv_cols(glo, r0, gtail)
            val, vtail = conv_cols(vlo, r0, vtail)
            h_ref[r0:r0 + sub, c * tf:(c + 1) * tf] = (gate * jax.nn.sigmoid(gate) * val).astype(BF16)
        tail_ref[:, glo:glo + tf] = gtail
        tail_ref[:, vlo:vlo + tf] = vtail

    y = jnp.dot(h_ref[...], wdn_ref[...], preferred_element_type=F32)
    o_ref[0] = _layer_norm(DEEPNORM_ALPHA * res_ref[0] + y, g_ref[...], b_ref[...])


def _ffn_kernel(x_ref, wup_ref, cw_ref, cb_ref, wdn_ref, g_ref, b_ref, o_ref, tail_ref, h_ref, xb_ref):
    @pl.when(pl.program_id(1) == 0)
    def _():
        tail_ref[...] = jnp.zeros_like(tail_ref)

    xb_ref[...] = x_ref[0].astype(BF16)
    _mlp_tile(x_ref, xb_ref, wup_ref, cw_ref, cb_ref, wdn_ref, g_ref, b_ref, o_ref, tail_ref, h_ref)


def _outproj_ffn_kernel(ym_ref, ya_ref, x_ref, wout_ref, gmix_ref, bmix_ref, wup_ref, cw_ref, cb_ref, wdn_ref,
                        g_ref, b_ref, o_ref, tail_ref, h_ref, xb_ref):
    tm = FFN_ROW_TILE

    @pl.when(pl.program_id(1) == 0)
    def _():
        tail_ref[...] = jnp.zeros_like(tail_ref)

    n_split = 4
    blk = tm // n_split
    ys = []
    for r in range(n_split):
        rs = slice(r * blk, (r + 1) * blk)
        ys.append(jnp.dot(ym_ref[0, rs, :], wout_ref[0:MLSTM_WIDTH, :], preferred_element_type=F32)
                  + jnp.dot(ya_ref[0, rs, :], wout_ref[MLSTM_WIDTH:, :], preferred_element_type=F32))
    for r in range(n_split):
        rs = slice(r * blk, (r + 1) * blk)
        x1 = _layer_norm(DEEPNORM_ALPHA * x_ref[0, rs, :] + ys[r], gmix_ref[...], bmix_ref[...])
        o_ref[0, rs, :] = x1
        xb_ref[rs, :] = x1.astype(BF16)
    _mlp_tile(o_ref, xb_ref, wup_ref, cw_ref, cb_ref, wdn_ref, g_ref, b_ref, o_ref, tail_ref, h_ref)


def _layer_spec(shape, layer):
    nd = len(shape) - 1
    return pl.BlockSpec((None,) + tuple(shape[1:]), lambda *_: (layer,) + (0,) * nd,
                        pipeline_mode=pl.Buffered(1))


def _ffn_call(body, name, bsz, s, row_inputs, const_inputs, layer, stacked):
    tm = FFN_ROW_TILE
    blk = lambda width: pl.BlockSpec((1, tm, width), lambda bi, si: (bi, si, 0))
    return pl.pallas_call(
        body,
        out_shape=jax.ShapeDtypeStruct((bsz, s, D_MODEL), F32),
        grid=(bsz, s // tm),
        in_specs=([blk(a.shape[-1]) for a in row_inputs] + [_const_spec(a.shape) for a in const_inputs]
                  + [_layer_spec(a.shape, layer) for a in stacked]),
        out_specs=blk(D_MODEL),
        scratch_shapes=[pltpu.VMEM((SUBLANES, 2 * FFN_DIM), F32),
                        pltpu.VMEM((tm, FFN_DIM), BF16),
                        pltpu.VMEM((tm, D_MODEL), BF16)],
        compiler_params=pltpu.CompilerParams(dimension_semantics=("parallel", "arbitrary"),
                                             vmem_limit_bytes=VMEM_LIMIT_BYTES),
        name=name,
    )(*row_inputs, *const_inputs, *stacked)


def _ffn(x3, layer, stacked):
    bsz, s, _ = x3.shape
    return _ffn_call(_ffn_kernel, "conv_ffn_ln", bsz, s, (x3,), (), layer, stacked)


def _outproj_ffn(ym3, ya3, x3, w_out, g_mix, b_mix, layer, stacked):
    bsz, s, _ = x3.shape
    return _ffn_call(_outproj_ffn_kernel, "outproj_ffn_ln", bsz, s, (ym3, ya3, x3), (w_out, g_mix, b_mix),
                     layer, stacked)


def _pool_kernel(x_ref, pw_ref, ls_ref, g_ref, b_ref, o_ref, xbuf_ref):
    tm = POOL_ROW_TILE
    si = pl.program_id(1)

    @pl.when(si == 0)
    def _():
        xbuf_ref[0:POOL_HALO, :] = jnp.zeros((POOL_HALO, D_MODEL), F32)

    xbuf_ref[POOL_HALO:, :] = x_ref[0]
    n_split = 4
    sub = tm // n_split
    ys = []
    for r in range(n_split):
        r0 = r * sub
        t_pos = si * tm + r0 + lax.broadcasted_iota(jnp.int32, (sub, 1), 0)
        parts = []
        for gi, w in enumerate(POOL_WINDOWS):
            cols = slice(gi * POOL_GROUP_DIM, (gi + 1) * POOL_GROUP_DIM)
            ext = xbuf_ref[r0:r0 + POOL_HALO + sub, cols]
            k = 1
            while k < w:
                ext = ext + pltpu.roll(ext, k, 0)
                k *= 2
            cur = xbuf_ref[POOL_HALO + r0:POOL_HALO + r0 + sub, cols]
            cnt = jnp.minimum(t_pos + 1, w).astype(F32)
            pooled = (ext[POOL_HALO:, :] / cnt - cur).astype(BF16)
            parts.append(jnp.dot(pooled, pw_ref[gi], preferred_element_type=F32))
        ys.append(jnp.concatenate(parts, axis=1) * ls_ref[...])
    for r in range(n_split):
        rs = slice(r * sub, (r + 1) * sub)
        o_ref[0, rs, :] = _layer_norm(DEEPNORM_ALPHA * x_ref[0, rs, :] + ys[r], g_ref[...], b_ref[...])
    xbuf_ref[0:POOL_HALO, :] = x_ref[0, tm - POOL_HALO:, :]


def _pool(x3, pool_w, layer_scale, g, b):
    bsz, s, _ = x3.shape
    tm = POOL_ROW_TILE
    blk = pl.BlockSpec((1, tm, D_MODEL), lambda bi, si: (bi, si, 0))
    return pl.pallas_call(
        _pool_kernel,
        out_shape=jax.ShapeDtypeStruct(x3.shape, F32),
        grid=(bsz, s // tm),
        in_specs=[blk, _const_spec(pool_w.shape), _const_spec(layer_scale.shape),
                  _const_spec(g.shape), _const_spec(b.shape)],
        out_specs=blk,
        scratch_shapes=[pltpu.VMEM((POOL_HALO + tm, D_MODEL), F32)],
        compiler_params=pltpu.CompilerParams(dimension_semantics=("parallel", "arbitrary"),
                                             vmem_limit_bytes=VMEM_LIMIT_BYTES),
        name="pool_ln",
    )(x3, pool_w, layer_scale, g, b)


def _rope_block_cols(w_rope):
    z = lambda n: jnp.zeros(w_rope.shape[:-1] + (n,), w_rope.dtype)
    return jnp.concatenate([z(ROPE_LO), w_rope[..., :ROPE_HALF], z(ROPE_HI - ROPE_LO - ROPE_HALF),
                            w_rope[..., ROPE_HALF:], z(LANES - ROPE_HI - ROPE_HALF)], axis=-1)


def _prep_w_in(w_in):
    offs = np.cumsum((0,) + IN_SIZES)
    parts = [w_in[:, offs[i]:offs[i + 1]] for i in range(len(IN_SIZES))]
    q_m, k_m, v_m, o_m, i_g, f_g, c_q, c_kv, k_r = parts
    gk = _rope_block_cols(k_r).at[:, 0:2 * MLSTM_HEADS].set(jnp.concatenate([i_g, f_g], axis=1))
    return jnp.concatenate([q_m, k_m, v_m, o_m, c_q, c_kv, gk], axis=1).astype(BF16)


def _prep_w_uq(w_uq):
    w = w_uq.reshape(Q_LORA, MLA_HEADS, HEAD_DIM + ROPE_DIM)
    blk = jnp.concatenate([w[..., :HEAD_DIM], _rope_block_cols(w[..., HEAD_DIM:])], axis=-1)
    return blk.reshape(Q_LORA, MLA_HEADS * QK_HEAD).astype(BF16)


def _prep_w_ukv(w_ukv):
    w = w_ukv.reshape(KV_LORA, MLA_HEADS, 2 * HEAD_DIM)
    k_nope = w[..., :HEAD_DIM].reshape(KV_LORA, MLA_WIDTH)
    v = w[..., HEAD_DIM:].reshape(KV_LORA, MLA_WIDTH)
    return jnp.concatenate([k_nope, v], axis=1).astype(BF16)


def _rope_tables():
    inv_freq = ROPE_THETA ** (-jnp.arange(0, ROPE_DIM, 2, dtype=F32) / ROPE_DIM)
    n_blk = ROW_TILE // LANES
    freq = jnp.tile(inv_freq, n_blk)[None, :]
    selc = np.zeros((n_blk, 2 * LANES, LANES), np.float32)
    sels = np.zeros((n_blk, 2 * LANES, LANES), np.float32)
    f = np.arange(ROPE_HALF)
    for k in range(n_blk):
        for half in range(2):
            rows = half * LANES + ROPE_HALF * k + f
            selc[k, rows, ROPE_LO + f] = 1.0
            selc[k, rows, ROPE_HI + f] = 1.0
            sels[k, rows, ROPE_LO + f] = -1.0
            sels[k, rows, ROPE_HI + f] = 1.0
    return freq, jnp.asarray(selc, BF16), jnp.asarray(sels, BF16)


def _pack_positions(positions):
    n_blk = ROW_TILE // LANES
    pos = positions.astype(F32).reshape(-1, n_blk, LANES).transpose(0, 2, 1)
    return jnp.repeat(pos, LANES // n_blk, axis=-1).reshape(-1, LANES)


def kernel(x, positions, even_w_in, even_b_igate, even_b_fgate, even_mlstm_norm, even_q_norm,
           even_kv_norm, even_w_uq, even_w_ukv, even_w_out, odd_pool_w, odd_layer_scale,
           ffn_w_up, ffn_conv_w, ffn_conv_b, ffn_w_down, ln_mix_g, ln_mix_b, ln_ffn_g, ln_ffn_b):
    bsz, s, d = x.shape
    t = bsz * s
    row = lambda v: v.reshape(1, -1)

    freq, selc, sels = _rope_tables()
    qm, km, vm, om, gates, qa, ka, va = _inproj(
        x.reshape(t, d), _pack_positions(positions), _prep_w_in(even_w_in[0]),
        row(even_q_norm[0]), row(even_kv_norm[0]), _prep_w_uq(even_w_uq[0]), _prep_w_ukv(even_w_ukv[0]),
        freq, selc, sels)

    gate_bias = jnp.concatenate([even_b_igate[0], even_b_fgate[0]])
    b3 = lambda a: a.reshape(bsz, s, a.shape[-1])
    ym = _mlstm(b3(qm), b3(km), b3(vm), b3(om), gates, gate_bias.reshape(-1, 1),
                row(even_mlstm_norm[0]), _mlstm_selector(), _mlstm_tri())
    ya = _attention(b3(qa), b3(ka), b3(va))

    stack_rows = lambda v: v.reshape(DEPTH, 1, -1)
    ffn_params = (ffn_w_up.astype(BF16), ffn_conv_w, stack_rows(ffn_conv_b), ffn_w_down.astype(BF16),
                  stack_rows(ln_ffn_g), stack_rows(ln_ffn_b))
    x1 = _outproj_ffn(ym, ya, x, even_w_out[0].astype(BF16), row(ln_mix_g[0]), row(ln_mix_b[0]), 0, ffn_params)

    x2 = _pool(x1, odd_pool_w[0].astype(BF16), row(odd_layer_scale[0]), row(ln_mix_g[1]), row(ln_mix_b[1]))
    return _ffn(x2, 1, ffn_params)
```

```python
import jax
import jax.numpy as jnp
import numpy as np
from jax import lax
from jax.experimental import pallas as pl
from jax.experimental.pallas import tpu as pltpu

F32 = jnp.float32
BF16 = jnp.bfloat16

D_MODEL = 1024
DEPTH = 2
MLSTM_HEADS = 4
HEAD_DIM = 128
MLSTM_WIDTH = MLSTM_HEADS * HEAD_DIM
MLA_HEADS = 4
ROPE_DIM = 64
ROPE_HALF = ROPE_DIM // 2
Q_LORA = 256
KV_LORA = 128
MLA_WIDTH = MLA_HEADS * HEAD_DIM
ROPE_THETA = 10000.0
POOL_WINDOWS = (2, 4, 8, 16)
POOL_GROUP_DIM = D_MODEL // len(POOL_WINDOWS)
FFN_DIM = 2816
CONV_WIDTH = 3
LN_EPS = 1e-5
RMS_EPS = 1e-6
DEEPNORM_ALPHA = (2 * DEPTH) ** 0.25
IN_SIZES = (MLSTM_WIDTH, MLSTM_WIDTH, MLSTM_WIDTH, MLSTM_WIDTH, MLSTM_HEADS, MLSTM_HEADS,
            Q_LORA, KV_LORA, ROPE_DIM)

LANES = 128
SUBLANES = 8
MXU_DIM = 256
VMEM_LIMIT_BYTES = 56 * 1024 * 1024

ROPE_LO = 8
ROPE_HI = ROPE_LO + LANES // 2
QK_HEAD = 2 * LANES

ROW_TILE = 512
MLSTM_CHUNK = 256
MLSTM_BLOCK = 2048
ATTN_TILE = 512
FFN_COLS = 256
FFN_ROW_TILE = 1024
FFN_SUB_ROWS = 128
POOL_ROW_TILE = 1024
POOL_HALO = 16


def _const_spec(shape):
    nd = len(shape)
    return pl.BlockSpec(shape, lambda *_: (0,) * nd, pipeline_mode=pl.Buffered(1))


def _layer_norm(z, g, b):
    mu = jnp.mean(z, axis=-1, keepdims=True)
    d = z - mu
    var = jnp.mean(d * d, axis=-1, keepdims=True)
    return d * lax.rsqrt(var + LN_EPS) * g + b


def _rms_norm(z, g):
    return z * lax.rsqrt(jnp.mean(z * z, axis=-1, keepdims=True) + RMS_EPS) * g


def _log_sigmoid(z):
    return -(jnp.maximum(-z, 0.0) + jnp.log1p(jnp.exp(-jnp.abs(z))))


def _split3(z):
    hi = z.astype(BF16)
    r1 = z - hi.astype(F32)
    mid = r1.astype(BF16)
    lo = (r1 - mid.astype(F32)).astype(BF16)
    return hi, mid, lo


def _split2_lanes(z):
    hi = z.astype(BF16)
    return jnp.concatenate([hi, (z - hi.astype(F32)).astype(BF16)], axis=1)


def _inproj_kernel(x_ref, pos_ref, w_in_ref, qn_ref, kvn_ref, wuq_ref, wukv_ref, freq_ref, selc_ref,
                   sels_ref, qm_ref, km_ref, vm_ref, om_ref, gates_ref, qa_ref, ka_ref, va_ref):
    ang = pos_ref[...] * freq_ref[...]
    cos2 = _split2_lanes(jnp.cos(ang))
    sin2 = _split2_lanes(jnp.sin(ang))
    n_blk = ROW_TILE // LANES
    cosm = jnp.concatenate([jnp.dot(cos2, selc_ref[k], preferred_element_type=F32) for k in range(n_blk)], axis=0)
    sinm = jnp.concatenate([jnp.dot(sin2, sels_ref[k], preferred_element_type=F32) for k in range(n_blk)], axis=0)

    def rope(blk):
        return blk * cosm + pltpu.roll(blk, LANES // 2, 1) * sinm

    xb = x_ref[...].astype(BF16)

    def proj(lo, width):
        return jnp.dot(xb, w_in_ref[:, lo:lo + width], preferred_element_type=F32)

    w = MLSTM_WIDTH
    qm_ref[...] = proj(0, w).astype(BF16)
    km_ref[...] = (proj(w, w) * (HEAD_DIM ** -0.5)).astype(BF16)
    vm_ref[...] = proj(2 * w, w).astype(BF16)
    om_ref[...] = proj(3 * w, w)

    c_q = proj(4 * w, Q_LORA)
    ckv_gk = proj(4 * w + Q_LORA, KV_LORA + LANES)
    c_kv = ckv_gk[:, :KV_LORA]
    gk = ckv_gk[:, KV_LORA:]
    gates_ref[...] = gk.T[0:2 * MLSTM_HEADS, :]
    k_rope = rope(gk).astype(BF16)

    scale = (HEAD_DIM + ROPE_DIM) ** -0.5 * float(np.log2(np.e))
    q = jnp.dot(_rms_norm(c_q, qn_ref[...]).astype(BF16), wuq_ref[...], preferred_element_type=F32)
    kv = jnp.dot(_rms_norm(c_kv, kvn_ref[...]).astype(BF16), wukv_ref[...], preferred_element_type=F32)
    for h in range(MLA_HEADS):
        lo = h * QK_HEAD
        qa_ref[:, lo:lo + LANES] = (q[:, lo:lo + LANES] * scale).astype(BF16)
        qa_ref[:, lo + LANES:lo + QK_HEAD] = (rope(q[:, lo + LANES:lo + QK_HEAD]) * scale).astype(BF16)
        ka_ref[:, lo:lo + LANES] = kv[:, h * HEAD_DIM:(h + 1) * HEAD_DIM].astype(BF16)
        ka_ref[:, lo + LANES:lo + QK_HEAD] = k_rope
    va_ref[...] = kv[:, MLA_WIDTH:].astype(BF16)


def _inproj(x2, pos_packed, w_in_p, q_norm, kv_norm, wuq_p, wukv_p, freq, selc, sels):
    t = x2.shape[0]
    tm = ROW_TILE
    row = lambda width: pl.BlockSpec((tm, width), lambda i: (i, 0))
    out_shapes = (
        jax.ShapeDtypeStruct((t, MLSTM_WIDTH), BF16),
        jax.ShapeDtypeStruct((t, MLSTM_WIDTH), BF16),
        jax.ShapeDtypeStruct((t, MLSTM_WIDTH), BF16),
        jax.ShapeDtypeStruct((t, MLSTM_WIDTH), F32),
        jax.ShapeDtypeStruct((2 * MLSTM_HEADS, t), F32),
        jax.ShapeDtypeStruct((t, MLA_HEADS * QK_HEAD), BF16),
        jax.ShapeDtypeStruct((t, MLA_HEADS * QK_HEAD), BF16),
        jax.ShapeDtypeStruct((t, MLA_WIDTH), BF16),
    )
    return pl.pallas_call(
        _inproj_kernel,
        out_shape=out_shapes,
        grid=(t // tm,),
        in_specs=[row(D_MODEL), pl.BlockSpec((tm // 4, LANES), lambda i: (i, 0)),
                  _const_spec(w_in_p.shape), _const_spec(q_norm.shape),
                  _const_spec(kv_norm.shape), _const_spec(wuq_p.shape), _const_spec(wukv_p.shape),
                  _const_spec(freq.shape), _const_spec(selc.shape), _const_spec(sels.shape)],
        out_specs=tuple(pl.BlockSpec((2 * MLSTM_HEADS, tm), lambda i: (0, i)) if s.shape[0] != t else row(s.shape[1])
                        for s in out_shapes),
        compiler_params=pltpu.CompilerParams(dimension_semantics=("parallel",),
                                             vmem_limit_bytes=VMEM_LIMIT_BYTES),
        name="inproj",
    )(x2, pos_packed, w_in_p, q_norm, kv_norm, wuq_p, wukv_p, freq, selc, sels)


def _mlstm_kernel(q_ref, k_ref, v_ref, o_ref, gt_ref, brow_ref, nw_ref, sel_ref, tri_ref, y_ref, ct_ref, m_ref,
                  a8_ref, xt_ref):
    L = MLSTM_CHUNK
    d = HEAD_DIM
    nh = MLSTM_HEADS
    heads = range(nh)

    @pl.when(pl.program_id(1) == 0)
    def _():
        ct_ref[...] = jnp.zeros_like(ct_ref)
        m_ref[...] = jnp.zeros_like(m_ref)

    causal = lax.broadcasted_iota(jnp.int32, (L, L), 0) >= lax.broadcasted_iota(jnp.int32, (L, L), 1)
    top_rows = lax.broadcasted_iota(jnp.int32, (2 * nh, L), 0) < nh
    ones_blk = jnp.ones((L, d), BF16)
    neg_inf = jnp.float32(-jnp.inf)

    def hs(h):
        return slice(h * d, (h + 1) * d)

    def lanes(j):
        return slice(j * LANES, (j + 1) * LANES)

    def scalar_part(c, slot):
        g8 = gt_ref[:, pl.ds(pl.multiple_of(c * L, L), L)] + brow_ref[...]
        b8 = sum(jnp.dot(p, tri_ref[...], preferred_element_type=F32) for p in _split3(_log_sigmoid(g8)))
        a8 = g8 - pltpu.roll(b8, nh, 0)
        w8 = jnp.exp(a8 - jnp.max(a8, axis=-1, keepdims=True))
        r = jnp.where(top_rows, w8, b8)
        rpad = jnp.concatenate([p.astype(F32) for p in _split3(r)]
                               + [jnp.zeros((LANES - 3 * 2 * nh, L), F32)], axis=0)
        a8_ref[slot] = a8
        xt_ref[slot] = rpad.T.astype(BF16)

    def local_part(c, slot):
        rows = pl.ds(pl.multiple_of(c * L, L), L)
        a8 = a8_ref[slot]
        bc = jnp.dot(xt_ref[slot], sel_ref[...], preferred_element_type=F32)
        q = [q_ref[0, rows, hs(h)] for h in heads]
        k = [k_ref[0, rows, hs(h)] for h in heads]
        s = [lax.dot_general(q[h], k[h], (((1,), (1,)), ((), ())), preferred_element_type=F32) for h in heads]
        a_low = [jnp.where(causal, a8[h:h + 1, :], neg_inf) for h in heads]
        mp = [jnp.max(a_low[h], axis=-1, keepdims=True) for h in heads]
        a_mat = [(s[h] * jnp.exp(a_low[h] - mp[h])).astype(BF16) for h in heads]
        v_aug = [jnp.concatenate([v_ref[0, rows, hs(h)], ones_blk], axis=1) for h in heads]
        intra = [jnp.dot(a_mat[h], v_aug[h], preferred_element_type=F32) for h in heads]
        w_rep = [bc[:, lanes(h)] for h in heads]
        b_rep = [bc[:, lanes(nh + h)] for h in heads]
        wv = [jnp.concatenate([(v_aug[h][:, :d].astype(F32) * w_rep[h]).astype(BF16), w_rep[h].astype(BF16)], axis=1)
              for h in heads]
        upd = [lax.dot_general(k[h], wv[h], (((0,), (0,)), ((), ())), preferred_element_type=F32) for h in heads]
        return rows, q, mp, b_rep, intra, upd

    def carried_part(local):
        rows, q, mp, b_rep, intra, upd = local
        ct = [ct_ref[h] for h in heads]
        inter = [jnp.dot(q[h], ct[h].astype(BF16), preferred_element_type=F32) for h in heads]
        m_prev = [m_ref[h][0:1, :] for h in heads]
        for h in heads:
            mp_rep = jnp.broadcast_to(mp[h], (L, LANES))
            m_rep = jnp.maximum(mp_rep, m_prev[h])
            e_intra = jnp.exp(mp_rep - m_rep)
            e_inter = jnp.exp(m_prev[h] - m_rep)
            num = e_intra * intra[h][:, :d] + e_inter * inter[h][:, :d]
            den = e_intra * intra[h][:, d:] + e_inter * inter[h][:, d:]
            hraw = num / jnp.maximum(jnp.abs(den), jnp.exp(-(b_rep[h] + m_rep)))
            gate = jax.nn.sigmoid(o_ref[0, rows, hs(h)])
            y_ref[0, rows, hs(h)] = (_rms_norm(hraw, nw_ref[:, hs(h)]) * gate).astype(BF16)
            mp_last = mp_rep[L - 1:L, :]
            m_last = jnp.maximum(mp_last, m_prev[h])
            keep = jnp.exp(m_prev[h] - m_last)
            add = jnp.exp(mp_last - m_last)
            ct_ref[h] = (jnp.concatenate([keep, keep], axis=1) * ct[h]
                         + jnp.concatenate([add, add], axis=1) * upd[h])
            m_ref[h] = jnp.broadcast_to(b_rep[h][L - 1:L, :] + m_last, (SUBLANES, LANES))

    n_pairs = MLSTM_BLOCK // (2 * L)
    scalar_part(0, 0)
    scalar_part(1, 1)

    def pair(i, carry):
        first = local_part(2 * i, 0)
        second = local_part(2 * i + 1, 1)
        carried_part(first)
        carried_part(second)
        nxt = jnp.minimum(i + 1, n_pairs - 1)
        scalar_part(2 * nxt, 0)
        scalar_part(2 * nxt + 1, 1)
        return carry

    lax.fori_loop(0, n_pairs, pair, 0)


def _mlstm(qm, km, vm, om, gates_t, b_row, norm_w, sel, tri_t):
    b, s, _ = qm.shape
    tb = MLSTM_BLOCK
    blk = lambda width: pl.BlockSpec((1, tb, width), lambda bi, si: (bi, si, 0))
    return pl.pallas_call(
        _mlstm_kernel,
        out_shape=jax.ShapeDtypeStruct((b, s, MLSTM_WIDTH), BF16),
        grid=(b, s // tb),
        in_specs=[blk(MLSTM_WIDTH), blk(MLSTM_WIDTH), blk(MLSTM_WIDTH), blk(MLSTM_WIDTH),
                  pl.BlockSpec((2 * MLSTM_HEADS, tb), lambda bi, si: (0, bi * (s // tb) + si)),
                  _const_spec(b_row.shape), _const_spec(norm_w.shape), _const_spec(sel.shape),
                  _const_spec(tri_t.shape)],
        out_specs=blk(MLSTM_WIDTH),
        scratch_shapes=[pltpu.VMEM((MLSTM_HEADS, HEAD_DIM, 2 * HEAD_DIM), F32),
                        pltpu.VMEM((MLSTM_HEADS, SUBLANES, LANES), F32),
                        pltpu.VMEM((2, 2 * MLSTM_HEADS, MLSTM_CHUNK), F32),
                        pltpu.VMEM((2, MLSTM_CHUNK, LANES), BF16)],
        compiler_params=pltpu.CompilerParams(dimension_semantics=("parallel", "arbitrary"),
                                             vmem_limit_bytes=VMEM_LIMIT_BYTES),
        name="mlstm",
    )(qm, km, vm, om, gates_t, b_row, norm_w, sel, tri_t)


def _mlstm_tri():
    idx = np.arange(MLSTM_CHUNK)
    return jnp.asarray(idx[:, None] <= idx[None, :], BF16)


def _mlstm_selector():
    r = np.arange(LANES)[:, None]
    c = np.arange(2 * MLSTM_HEADS * LANES)[None, :]
    return jnp.asarray((r < 3 * 2 * MLSTM_HEADS) & (r % (2 * MLSTM_HEADS) == c // LANES), BF16)


def _attn_kernel(q_ref, k_ref, v_ref, o_ref, s_ref, m_ref, acc_ref):
    t = ATTN_TILE
    nh = MLA_HEADS
    p = pl.program_id(1)
    half_tiles = pl.num_programs(1)
    diag = (lax.broadcasted_iota(jnp.int32, (t, t), 0) >= lax.broadcasted_iota(jnp.int32, (t, t), 1))
    neg_inf = jnp.float32(-jnp.inf)
    ones_blk = jnp.ones((t, HEAD_DIM), BF16)

    def reset():
        m_ref[...] = jnp.full(m_ref.shape, neg_inf, F32)
        acc_ref[...] = jnp.zeros_like(acc_ref)

    def scores(tile, j, slot, h):
        r0 = pl.multiple_of(j * t, t)
        q = q_ref[0, tile, :, h * QK_HEAD:(h + 1) * QK_HEAD]
        k = k_ref[0, pl.ds(r0, t), h * QK_HEAD:(h + 1) * QK_HEAD]
        s_ref[slot * nh + h] = lax.dot_general(q, k, (((1,), (1,)), ((), ())),
                                               preferred_element_type=F32)

    def accumulate(j, slot, h, masked):
        r0 = pl.multiple_of(j * t, t)
        v_aug = jnp.concatenate(
            [v_ref[0, pl.ds(r0, t), h * HEAD_DIM:(h + 1) * HEAD_DIM], ones_blk], axis=1)
        s = s_ref[slot * nh + h]
        if masked:
            s = jnp.where(diag, s, neg_inf)
        m_old = m_ref[h]
        m_new = jnp.maximum(m_old, jnp.max(s, axis=-1, keepdims=True))
        pr = jnp.exp2(s - jnp.concatenate([m_new] * (t // LANES), axis=1)).astype(BF16)
        alpha = jnp.exp2(m_old - m_new)
        acc_ref[h] = (jnp.concatenate([alpha, alpha], axis=1) * acc_ref[h]
                      + jnp.dot(pr, v_aug, preferred_element_type=F32))
        m_ref[h] = m_new

    def finish(tile, h):
        acc = acc_ref[h]
        o_ref[0, tile, :, h * HEAD_DIM:(h + 1) * HEAD_DIM] = (
            acc[:, :HEAD_DIM] / acc[:, HEAD_DIM:]).astype(BF16)

    def half_step(tile, j, slot_in, slot_out):
        for h in range(nh):
            scores(tile, j + 1, slot_out, h)
            accumulate(j, slot_in, h, masked=False)

    reset()
    for h in range(nh):
        scores(0, 0, 0, h)

    def body_a(i, carry):
        half_step(0, 2 * i, 0, 1)

        @pl.when(2 * i + 1 < p)
        def _():
            half_step(0, 2 * i + 1, 1, 0)
        return carry

    lax.fori_loop(0, (p + 1) // 2, body_a, 0)
    for parity in range(2):
        @pl.when(lax.rem(p, 2) == parity)
        def _(parity=parity):
            for h in range(nh):
                scores(1, 0, 2, h)
                accumulate(p, parity, h, masked=True)
                finish(0, h)

    reset()
    last = p + half_tiles
    half_step(1, 0, 2, 1)

    def body_b(i, carry):
        half_step(1, 2 * i + 1, 1, 0)

        @pl.when(2 * i + 2 < last)
        def _():
            half_step(1, 2 * i + 2, 0, 1)
        return carry

    lax.fori_loop(0, last // 2, body_b, 0)
    for h in range(nh):
        accumulate(last, lax.rem(last, 2), h, masked=True)
        finish(1, h)


def _attention(qa, ka, va):
    b, s, _ = qa.shape
    t = ATTN_TILE
    half = s // 2
    tile_pair = lambda width: pl.BlockSpec((1, 2, t, width), lambda bi, p: (bi, 0, p, 0))
    out = pl.pallas_call(
        _attn_kernel,
        out_shape=jax.ShapeDtypeStruct((b, 2, half, MLA_WIDTH), BF16),
        grid=(b, half // t),
        in_specs=[tile_pair(MLA_HEADS * QK_HEAD),
                  pl.BlockSpec((1, s, MLA_HEADS * QK_HEAD), lambda bi, p: (bi, 0, 0)),
                  pl.BlockSpec((1, s, MLA_WIDTH), lambda bi, p: (bi, 0, 0))],
        out_specs=tile_pair(MLA_WIDTH),
        scratch_shapes=[pltpu.VMEM((3 * MLA_HEADS, t, t), F32),
                        pltpu.VMEM((MLA_HEADS, t, LANES), F32),
                        pltpu.VMEM((MLA_HEADS, t, 2 * HEAD_DIM), F32)],
        compiler_params=pltpu.CompilerParams(dimension_semantics=("parallel", "arbitrary"),
                                             vmem_limit_bytes=VMEM_LIMIT_BYTES),
        name="mla_attention",
    )(qa.reshape(b, 2, half, MLA_HEADS * QK_HEAD), ka, va)
    return out.reshape(b, s, MLA_WIDTH)


def _shift_rows(u, tail, k):
    return pltpu.roll(jnp.concatenate([tail, u], axis=0), k, 0)[SUBLANES:]


def _mlp_tile(res_ref, xb_ref, wup_ref, cw_ref, cb_ref, wdn_ref, g_ref, b_ref, o_ref, tail_ref, h_ref):
    tm = FFN_ROW_TILE
    tf = FFN_COLS
    sub = FFN_SUB_ROWS
    def conv_cols(lo, r0, tail):
        u = jnp.dot(xb_ref[r0:r0 + sub, :], wup_ref[:, lo:lo + tf], preferred_element_type=F32)
        uc = (_shift_rows(u, tail, 2) * cw_ref[0:1, lo:lo + tf]
              + _shift_rows(u, tail, 1) * cw_ref[1:2, lo:lo + tf]
              + u * cw_ref[2:3, lo:lo + tf] + cb_ref[:, lo:lo + tf])
        return uc, u[sub - SUBLANES:, :]

    for c in range(FFN_DIM // tf):
        glo, vlo = c * tf, FFN_DIM + c * tf
        gtail = tail_ref[:, glo:glo + tf]
        vtail = tail_ref[:, vlo:vlo + tf]
        for r0 in range(0, tm, sub):
            gate, gtail = conv_cols(glo, r0, gtail)
            val, vtail = conv_cols(vlo, r0, vtail)
            h_ref[r0:r0 + sub, c * tf:(c + 1) * tf] = ((gate + gate * jnp.tanh(gate)) * val).astype(BF16)
        tail_ref[:, glo:glo + tf] = gtail
        tail_ref[:, vlo:vlo + tf] = vtail

    n_split = 4
    blk = tm // n_split
    ys = [jnp.dot(h_ref[r * blk:(r + 1) * blk, :], wdn_ref[...], preferred_element_type=F32)
          for r in range(n_split)]
    for r in range(n_split):
        rs = slice(r * blk, (r + 1) * blk)
        o_ref[0, rs, :] = _layer_norm(DEEPNORM_ALPHA * res_ref[0, rs, :] + ys[r], g_ref[...], b_ref[...])


def _ffn_kernel(x_ref, wup_ref, cw_ref, cb_ref, wdn_ref, g_ref, b_ref, o_ref, tail_ref, h_ref, xb_ref):
    @pl.when(pl.program_id(1) == 0)
    def _():
        tail_ref[...] = jnp.zeros_like(tail_ref)

    xb_ref[...] = x_ref[0].astype(BF16)
    _mlp_tile(x_ref, xb_ref, wup_ref, cw_ref, cb_ref, wdn_ref, g_ref, b_ref, o_ref, tail_ref, h_ref)


def _outproj_ffn_kernel(ym_ref, ya_ref, x_ref, wout_ref, gmix_ref, bmix_ref, wup_ref, cw_ref, cb_ref, wdn_ref,
                        g_ref, b_ref, o_ref, tail_ref, h_ref, xb_ref):
    tm = FFN_ROW_TILE

    @pl.when(pl.program_id(1) == 0)
    def _():
        tail_ref[...] = jnp.zeros_like(tail_ref)

    n_split = 4
    blk = tm // n_split
    ys = []
    for r in range(n_split):
        rs = slice(r * blk, (r + 1) * blk)
        ys.append(jnp.dot(ym_ref[0, rs, :], wout_ref[0:MLSTM_WIDTH, :], preferred_element_type=F32)
                  + jnp.dot(ya_ref[0, rs, :], wout_ref[MLSTM_WIDTH:, :], preferred_element_type=F32))
    for r in range(n_split):
        rs = slice(r * blk, (r + 1) * blk)
        x1 = _layer_norm(DEEPNORM_ALPHA * x_ref[0, rs, :] + ys[r], gmix_ref[...], bmix_ref[...])
        o_ref[0, rs, :] = x1
        xb_ref[rs, :] = x1.astype(BF16)
    _mlp_tile(o_ref, xb_ref, wup_ref, cw_ref, cb_ref, wdn_ref, g_ref, b_ref, o_ref, tail_ref, h_ref)


def _layer_spec(shape, layer):
    nd = len(shape) - 1
    return pl.BlockSpec((None,) + tuple(shape[1:]), lambda *_: (layer,) + (0,) * nd,
                        pipeline_mode=pl.Buffered(1))


def _ffn_call(body, name, bsz, s, row_inputs, const_inputs, layer, stacked):
    tm = FFN_ROW_TILE
    blk = lambda width: pl.BlockSpec((1, tm, width), lambda bi, si: (bi, si, 0))
    return pl.pallas_call(
        body,
        out_shape=jax.ShapeDtypeStruct((bsz, s, D_MODEL), F32),
        grid=(bsz, s // tm),
        in_specs=([blk(a.shape[-1]) for a in row_inputs] + [_const_spec(a.shape) for a in const_inputs]
                  + [_layer_spec(a.shape, layer) for a in stacked]),
        out_specs=blk(D_MODEL),
        scratch_shapes=[pltpu.VMEM((SUBLANES, 2 * FFN_DIM), F32),
                        pltpu.VMEM((tm, FFN_DIM), BF16),
                        pltpu.VMEM((tm, D_MODEL), BF16)],
        compiler_params=pltpu.CompilerParams(dimension_semantics=("parallel", "arbitrary"),
                                             vmem_limit_bytes=VMEM_LIMIT_BYTES),
        name=name,
    )(*row_inputs, *const_inputs, *stacked)


def _ffn(x3, layer, stacked):
    bsz, s, _ = x3.shape
    return _ffn_call(_ffn_kernel, "conv_ffn_ln", bsz, s, (x3,), (), layer, stacked)


def _outproj_ffn(ym3, ya3, x3, w_out, g_mix, b_mix, layer, stacked):
    bsz, s, _ = x3.shape
    return _ffn_call(_outproj_ffn_kernel, "outproj_ffn_ln", bsz, s, (ym3, ya3, x3), (w_out, g_mix, b_mix),
                     layer, stacked)


def _pool_kernel(x_ref, pw_ref, ls_ref, g_ref, b_ref, o_ref, xbuf_ref):
    tm = POOL_ROW_TILE
    si = pl.program_id(1)

    @pl.when(si == 0)
    def _():
        xbuf_ref[0:POOL_HALO, :] = jnp.zeros((POOL_HALO, D_MODEL), F32)

    xbuf_ref[POOL_HALO:, :] = x_ref[0]
    n_split = 4
    sub = tm // n_split
    ys = []
    for r in range(n_split):
        r0 = r * sub
        t_pos = si * tm + r0 + lax.broadcasted_iota(jnp.int32, (sub, 1), 0)
        parts = []
        for gi, w in enumerate(POOL_WINDOWS):
            cols = slice(gi * POOL_GROUP_DIM, (gi + 1) * POOL_GROUP_DIM)
            ext = xbuf_ref[r0:r0 + POOL_HALO + sub, cols]
            k = 1
            while k < w:
                ext = ext + pltpu.roll(ext, k, 0)
                k *= 2
            cur = xbuf_ref[POOL_HALO + r0:POOL_HALO + r0 + sub, cols]
            cnt = jnp.minimum(t_pos + 1, w).astype(F32)
            pooled = (ext[POOL_HALO:, :] / cnt - cur).astype(BF16)
            parts.append(jnp.dot(pooled, pw_ref[gi], preferred_element_type=F32))
        ys.append(jnp.concatenate(parts, axis=1) * ls_ref[...])
    for r in range(n_split):
        rs = slice(r * sub, (r + 1) * sub)
        o_ref[0, rs, :] = _layer_norm(DEEPNORM_ALPHA * x_ref[0, rs, :] + ys[r], g_ref[...], b_ref[...])
    xbuf_ref[0:POOL_HALO, :] = x_ref[0, tm - POOL_HALO:, :]


def _pool(x3, pool_w, layer_scale, g, b):
    bsz, s, _ = x3.shape
    tm = POOL_ROW_TILE
    blk = pl.BlockSpec((1, tm, D_MODEL), lambda bi, si: (bi, si, 0))
    return pl.pallas_call(
        _pool_kernel,
        out_shape=jax.ShapeDtypeStruct(x3.shape, F32),
        grid=(bsz, s // tm),
        in_specs=[blk, _const_spec(pool_w.shape), _const_spec(layer_scale.shape),
                  _const_spec(g.shape), _const_spec(b.shape)],
        out_specs=blk,
        scratch_shapes=[pltpu.VMEM((POOL_HALO + tm, D_MODEL), F32)],
        compiler_params=pltpu.CompilerParams(dimension_semantics=("parallel", "arbitrary"),
                                             vmem_limit_bytes=VMEM_LIMIT_BYTES),
        name="pool_ln",
    )(x3, pool_w, layer_scale, g, b)


def _rope_block_cols(w_rope):
    z = lambda n: jnp.zeros(w_rope.shape[:-1] + (n,), w_rope.dtype)
    return jnp.concatenate([z(ROPE_LO), w_rope[..., :ROPE_HALF], z(ROPE_HI - ROPE_LO - ROPE_HALF),
                            w_rope[..., ROPE_HALF:], z(LANES - ROPE_HI - ROPE_HALF)], axis=-1)


def _prep_w_in(w_in):
    offs = np.cumsum((0,) + IN_SIZES)
    parts = [w_in[:, offs[i]:offs[i + 1]] for i in range(len(IN_SIZES))]
    q_m, k_m, v_m, o_m, i_g, f_g, c_q, c_kv, k_r = parts
    gk = _rope_block_cols(k_r).at[:, 0:2 * MLSTM_HEADS].set(jnp.concatenate([i_g, f_g], axis=1))
    return jnp.concatenate([q_m, k_m, v_m, o_m, c_q, c_kv, gk], axis=1).astype(BF16)


def _prep_w_uq(w_uq):
    w = w_uq.reshape(Q_LORA, MLA_HEADS, HEAD_DIM + ROPE_DIM)
    blk = jnp.concatenate([w[..., :HEAD_DIM], _rope_block_cols(w[..., HEAD_DIM:])], axis=-1)
    return blk.reshape(Q_LORA, MLA_HEADS * QK_HEAD).astype(BF16)


def _prep_w_ukv(w_ukv):
    w = w_ukv.reshape(KV_LORA, MLA_HEADS, 2 * HEAD_DIM)
    k_nope = w[..., :HEAD_DIM].reshape(KV_LORA, MLA_WIDTH)
    v = w[..., HEAD_DIM:].reshape(KV_LORA, MLA_WIDTH)
    return jnp.concatenate([k_nope, v], axis=1).astype(BF16)


def _rope_tables():
    inv_freq = ROPE_THETA ** (-jnp.arange(0, ROPE_DIM, 2, dtype=F32) / ROPE_DIM)
    n_blk = ROW_TILE // LANES
    freq = jnp.tile(inv_freq, n_blk)[None, :]
    selc = np.zeros((n_blk, 2 * LANES, LANES), np.float32)
    sels = np.zeros((n_blk, 2 * LANES, LANES), np.float32)
    f = np.arange(ROPE_HALF)
    for k in range(n_blk):
        for half in range(2):
            rows = half * LANES + ROPE_HALF * k + f
            selc[k, rows, ROPE_LO + f] = 1.0
            selc[k, rows, ROPE_HI + f] = 1.0
            sels[k, rows, ROPE_LO + f] = -1.0
            sels[k, rows, ROPE_HI + f] = 1.0
    return freq, jnp.asarray(selc, BF16), jnp.asarray(sels, BF16)


def _pack_positions(positions):
    n_blk = ROW_TILE // LANES
    pos = positions.astype(F32).reshape(-1, n_blk, LANES).transpose(0, 2, 1)
    return jnp.repeat(pos, LANES // n_blk, axis=-1).reshape(-1, LANES)


def kernel(x, positions, even_w_in, even_b_igate, even_b_fgate, even_mlstm_norm, even_q_norm,
           even_kv_norm, even_w_uq, even_w_ukv, even_w_out, odd_pool_w, odd_layer_scale,
           ffn_w_up, ffn_conv_w, ffn_conv_b, ffn_w_down, ln_mix_g, ln_mix_b, ln_ffn_g, ln_ffn_b):
    bsz, s, d = x.shape
    t = bsz * s
    row = lambda v: v.reshape(1, -1)

    freq, selc, sels = _rope_tables()
    qm, km, vm, om, gates, qa, ka, va = _inproj(
        x.reshape(t, d), _pack_positions(positions), _prep_w_in(even_w_in[0]),
        row(even_q_norm[0]), row(even_kv_norm[0]), _prep_w_uq(even_w_uq[0]), _prep_w_ukv(even_w_ukv[0]),
        freq, selc, sels)

    gate_bias = jnp.concatenate([even_b_igate[0], even_b_fgate[0]])
    b3 = lambda a: a.reshape(bsz, s, a.shape[-1])
    ym = _mlstm(b3(qm), b3(km), b3(vm), b3(om), gates, gate_bias.reshape(-1, 1),
                row(even_mlstm_norm[0]), _mlstm_selector(), _mlstm_tri())
    ya = _attention(b3(qa), b3(ka), b3(va))

    stack_rows = lambda v: v.reshape(DEPTH, 1, -1)
    gate_half = jnp.concatenate([jnp.full((FFN_DIM,), 0.5, F32), jnp.ones((FFN_DIM,), F32)])
    ffn_params = (ffn_w_up.astype(BF16), ffn_conv_w * gate_half, stack_rows(ffn_conv_b * gate_half),
                  ffn_w_down.astype(BF16), stack_rows(ln_ffn_g), stack_rows(ln_ffn_b))
    x1 = _outproj_ffn(ym, ya, x, even_w_out[0].astype(BF16), row(ln_mix_g[0]), row(ln_mix_b[0]), 0, ffn_params)

    x2 = _pool(x1, odd_pool_w[0].astype(BF16), row(odd_layer_scale[0]), row(ln_mix_g[1]), row(ln_mix_b[1]))
    return _ffn(x2, 1, ffn_params)
```

```python
import jax
import jax.numpy as jnp
import numpy as np
from jax import lax
from jax.experimental import pallas as pl
from jax.experimental.pallas import tpu as pltpu

F32 = jnp.float32
BF16 = jnp.bfloat16

D_MODEL = 1024
DEPTH = 2
MLSTM_HEADS = 4
HEAD_DIM = 128
MLSTM_WIDTH = MLSTM_HEADS * HEAD_DIM
MLA_HEADS = 4
ROPE_DIM = 64
ROPE_HALF = ROPE_DIM // 2
Q_LORA = 256
KV_LORA = 128
MLA_WIDTH = MLA_HEADS * HEAD_DIM
ROPE_THETA = 10000.0
POOL_WINDOWS = (2, 4, 8, 16)
POOL_GROUP_DIM = D_MODEL // len(POOL_WINDOWS)
FFN_DIM = 2816
CONV_WIDTH = 3
LN_EPS = 1e-5
RMS_EPS = 1e-6
DEEPNORM_ALPHA = (2 * DEPTH) ** 0.25
IN_SIZES = (MLSTM_WIDTH, MLSTM_WIDTH, MLSTM_WIDTH, MLSTM_WIDTH, MLSTM_HEADS, MLSTM_HEADS,
            Q_LORA, KV_LORA, ROPE_DIM)

LANES = 128
SUBLANES = 8
VMEM_LIMIT_BYTES = 56 * 1024 * 1024

ROPE_LO = 8
ROPE_HI = ROPE_LO + LANES // 2
QK_HEAD = 2 * LANES

ROW_TILE = 512
MLSTM_CHUNK = 256
MLSTM_BLOCK = 2048
ATTN_TILE = 512
FFN_COLS = 256
FFN_ROW_TILE = 1024
FFN_SUB_ROWS = 128
POOL_ROW_TILE = 1024
ROW_BLOCKS = 4
POOL_HALO = 16


def _const_spec(shape):
    nd = len(shape)
    return pl.BlockSpec(shape, lambda *_: (0,) * nd, pipeline_mode=pl.Buffered(1))


def _layer_norm(z, g, b):
    mu = jnp.mean(z, axis=-1, keepdims=True)
    d = z - mu
    var = jnp.mean(d * d, axis=-1, keepdims=True)
    return d * lax.rsqrt(var + LN_EPS) * g + b


def _rms_norm(z, g):
    return z * lax.rsqrt(jnp.mean(z * z, axis=-1, keepdims=True) + RMS_EPS) * g


def _log_sigmoid(z):
    return -(jnp.maximum(-z, 0.0) + jnp.log1p(jnp.exp(-jnp.abs(z))))


def _split3(z):
    hi = z.astype(BF16)
    r1 = z - hi.astype(F32)
    mid = r1.astype(BF16)
    lo = (r1 - mid.astype(F32)).astype(BF16)
    return hi, mid, lo


def _split2_lanes(z):
    hi = z.astype(BF16)
    return jnp.concatenate([hi, (z - hi.astype(F32)).astype(BF16)], axis=1)


def _inproj_kernel(x_ref, pos_ref, w_in_ref, qn_ref, kvn_ref, wuq_ref, wukv_ref, freq_ref, selc_ref,
                   sels_ref, qm_ref, km_ref, vm_ref, om_ref, gates_ref, qa_ref, ka_ref, va_ref):
    ang = pos_ref[...] * freq_ref[...]
    cos2 = _split2_lanes(jnp.cos(ang))
    sin2 = _split2_lanes(jnp.sin(ang))
    n_blk = ROW_TILE // LANES
    cosm = jnp.concatenate([jnp.dot(cos2, selc_ref[k], preferred_element_type=F32) for k in range(n_blk)], axis=0)
    sinm = jnp.concatenate([jnp.dot(sin2, sels_ref[k], preferred_element_type=F32) for k in range(n_blk)], axis=0)

    def rope(blk):
        return blk * cosm + pltpu.roll(blk, LANES // 2, 1) * sinm

    xb = x_ref[...].astype(BF16)

    def proj(lo, width):
        return jnp.dot(xb, w_in_ref[:, lo:lo + width], preferred_element_type=F32)

    w = MLSTM_WIDTH
    qm_ref[...] = proj(0, w).astype(BF16)
    km_ref[...] = (proj(w, w) * (HEAD_DIM ** -0.5)).astype(BF16)
    vm_ref[...] = proj(2 * w, w).astype(BF16)
    om_ref[...] = proj(3 * w, w)

    c_q = proj(4 * w, Q_LORA)
    ckv_gk = proj(4 * w + Q_LORA, KV_LORA + LANES)
    c_kv = ckv_gk[:, :KV_LORA]
    gk = ckv_gk[:, KV_LORA:]
    gates_ref[...] = gk.T[0:2 * MLSTM_HEADS, :]
    k_rope = rope(gk).astype(BF16)

    scale = (HEAD_DIM + ROPE_DIM) ** -0.5 * float(np.log2(np.e))
    q = jnp.dot(_rms_norm(c_q, qn_ref[...]).astype(BF16), wuq_ref[...], preferred_element_type=F32)
    kv = jnp.dot(_rms_norm(c_kv, kvn_ref[...]).astype(BF16), wukv_ref[...], preferred_element_type=F32)
    for h in range(MLA_HEADS):
        lo = h * QK_HEAD
        qa_ref[:, lo:lo + LANES] = (q[:, lo:lo + LANES] * scale).astype(BF16)
        qa_ref[:, lo + LANES:lo + QK_HEAD] = (rope(q[:, lo + LANES:lo + QK_HEAD]) * scale).astype(BF16)
        ka_ref[:, lo:lo + LANES] = kv[:, h * HEAD_DIM:(h + 1) * HEAD_DIM].astype(BF16)
        ka_ref[:, lo + LANES:lo + QK_HEAD] = k_rope
    va_ref[...] = kv[:, MLA_WIDTH:].astype(BF16)


def _inproj(x2, pos_packed, w_in_p, q_norm, kv_norm, wuq_p, wukv_p, freq, selc, sels):
    t = x2.shape[0]
    tm = ROW_TILE
    row = lambda width: pl.BlockSpec((tm, width), lambda i: (i, 0))
    out_shapes = (
        jax.ShapeDtypeStruct((t, MLSTM_WIDTH), BF16),
        jax.ShapeDtypeStruct((t, MLSTM_WIDTH), BF16),
        jax.ShapeDtypeStruct((t, MLSTM_WIDTH), BF16),
        jax.ShapeDtypeStruct((t, MLSTM_WIDTH), F32),
        jax.ShapeDtypeStruct((2 * MLSTM_HEADS, t), F32),
        jax.ShapeDtypeStruct((t, MLA_HEADS * QK_HEAD), BF16),
        jax.ShapeDtypeStruct((t, MLA_HEADS * QK_HEAD), BF16),
        jax.ShapeDtypeStruct((t, MLA_WIDTH), BF16),
    )
    return pl.pallas_call(
        _inproj_kernel,
        out_shape=out_shapes,
        grid=(t // tm,),
        in_specs=[row(D_MODEL), pl.BlockSpec((tm // 4, LANES), lambda i: (i, 0)),
                  _const_spec(w_in_p.shape), _const_spec(q_norm.shape),
                  _const_spec(kv_norm.shape), _const_spec(wuq_p.shape), _const_spec(wukv_p.shape),
                  _const_spec(freq.shape), _const_spec(selc.shape), _const_spec(sels.shape)],
        out_specs=tuple(pl.BlockSpec((2 * MLSTM_HEADS, tm), lambda i: (0, i)) if s.shape[0] != t else row(s.shape[1])
                        for s in out_shapes),
        compiler_params=pltpu.CompilerParams(dimension_semantics=("parallel",),
                                             vmem_limit_bytes=VMEM_LIMIT_BYTES),
        name="inproj",
    )(x2, pos_packed, w_in_p, q_norm, kv_norm, wuq_p, wukv_p, freq, selc, sels)


def _mlstm_kernel(q_ref, k_ref, v_ref, o_ref, gt_ref, brow_ref, nw_ref, sel_ref, tri_ref, y_ref, ct_ref, m_ref,
                  a8_ref, xt_ref):
    L = MLSTM_CHUNK
    d = HEAD_DIM
    nh = MLSTM_HEADS
    heads = range(nh)

    @pl.when(pl.program_id(1) == 0)
    def _():
        ct_ref[...] = jnp.zeros_like(ct_ref)
        m_ref[...] = jnp.zeros_like(m_ref)

    causal = lax.broadcasted_iota(jnp.int32, (L, L), 0) >= lax.broadcasted_iota(jnp.int32, (L, L), 1)
    top_rows = lax.broadcasted_iota(jnp.int32, (2 * nh, L), 0) < nh
    ones_blk = jnp.ones((L, d), BF16)
    neg_inf = jnp.float32(-jnp.inf)

    def hs(h):
        return slice(h * d, (h + 1) * d)

    def lanes(j):
        return slice(j * LANES, (j + 1) * LANES)

    def scalar_part(c, slot):
        g8 = gt_ref[:, pl.ds(pl.multiple_of(c * L, L), L)] + brow_ref[...]
        b8 = sum(jnp.dot(p, tri_ref[...], preferred_element_type=F32) for p in _split3(_log_sigmoid(g8)))
        a8 = g8 - pltpu.roll(b8, nh, 0)
        w8 = jnp.exp(a8 - jnp.max(a8, axis=-1, keepdims=True))
        r = jnp.where(top_rows, w8, b8)
        rpad = jnp.concatenate([p.astype(F32) for p in _split3(r)]
                               + [jnp.zeros((LANES - 3 * 2 * nh, L), F32)], axis=0)
        a8_ref[slot] = a8
        xt_ref[slot] = rpad.T.astype(BF16)

    def local_part(c, slot):
        rows = pl.ds(pl.multiple_of(c * L, L), L)
        a8 = a8_ref[slot]
        bc = jnp.dot(xt_ref[slot], sel_ref[...], preferred_element_type=F32)
        q = [q_ref[0, rows, hs(h)] for h in heads]
        k = [k_ref[0, rows, hs(h)] for h in heads]
        s = [lax.dot_general(q[h], k[h], (((1,), (1,)), ((), ())), preferred_element_type=F32) for h in heads]
        a_low = [jnp.where(causal, a8[h:h + 1, :], neg_inf) for h in heads]
        mp = [jnp.max(a_low[h], axis=-1, keepdims=True) for h in heads]
        a_mat = [(s[h] * jnp.exp(a_low[h] - mp[h])).astype(BF16) for h in heads]
        v_aug = [jnp.concatenate([v_ref[0, rows, hs(h)], ones_blk], axis=1) for h in heads]
        intra = [jnp.dot(a_mat[h], v_aug[h], preferred_element_type=F32) for h in heads]
        w_rep = [bc[:, lanes(h)] for h in heads]
        b_rep = [bc[:, lanes(nh + h)] for h in heads]
        wv = [jnp.concatenate([(v_aug[h][:, :d].astype(F32) * w_rep[h]).astype(BF16), w_rep[h].astype(BF16)], axis=1)
              for h in heads]
        upd = [lax.dot_general(k[h], wv[h], (((0,), (0,)), ((), ())), preferred_element_type=F32) for h in heads]
        return rows, q, mp, b_rep, intra, upd

    def carried_part(local):
        rows, q, mp, b_rep, intra, upd = local
        ct = [ct_ref[h] for h in heads]
        inter = [jnp.dot(q[h], ct[h].astype(BF16), preferred_element_type=F32) for h in heads]
        m_prev = [m_ref[h][0:1, :] for h in heads]
        for h in heads:
            mp_rep = jnp.broadcast_to(mp[h], (L, LANES))
            m_rep = jnp.maximum(mp_rep, m_prev[h])
            e_intra = jnp.exp(mp_rep - m_rep)
            e_inter = jnp.exp(m_prev[h] - m_rep)
            num = e_intra * intra[h][:, :d] + e_inter * inter[h][:, :d]
            den = e_intra * intra[h][:, d:] + e_inter * inter[h][:, d:]
            hraw = num / jnp.maximum(jnp.abs(den), jnp.exp(-(b_rep[h] + m_rep)))
            gate = jax.nn.sigmoid(o_ref[0, rows, hs(h)])
            y_ref[0, rows, hs(h)] = (_rms_norm(hraw, nw_ref[:, hs(h)]) * gate).astype(BF16)
            mp_last = mp_rep[L - 1:L, :]
            m_last = jnp.maximum(mp_last, m_prev[h])
            keep = jnp.exp(m_prev[h] - m_last)
            add = jnp.exp(mp_last - m_last)
            ct_ref[h] = (jnp.concatenate([keep, keep], axis=1) * ct[h]
                         + jnp.concatenate([add, add], axis=1) * upd[h])
            m_ref[h] = jnp.broadcast_to(b_rep[h][L - 1:L, :] + m_last, (SUBLANES, LANES))

    n_pairs = MLSTM_BLOCK // (2 * L)
    scalar_part(0, 0)
    scalar_part(1, 1)

    def pair(i, carry):
        first = local_part(2 * i, 0)
        second = local_part(2 * i + 1, 1)
        carried_part(first)
        carried_part(second)
        nxt = jnp.minimum(i + 1, n_pairs - 1)
        scalar_part(2 * nxt, 0)
        scalar_part(2 * nxt + 1, 1)
        return carry

    lax.fori_loop(0, n_pairs, pair, 0)


def _mlstm(qm, km, vm, om, gates_t, b_row, norm_w, sel, tri_t):
    b, s, _ = qm.shape
    tb = MLSTM_BLOCK
    blk = lambda width: pl.BlockSpec((1, tb, width), lambda bi, si: (bi, si, 0))
    return pl.pallas_call(
        _mlstm_kernel,
        out_shape=jax.ShapeDtypeStruct((b, s, MLSTM_WIDTH), BF16),
        grid=(b, s // tb),
        in_specs=[blk(MLSTM_WIDTH), blk(MLSTM_WIDTH), blk(MLSTM_WIDTH), blk(MLSTM_WIDTH),
                  pl.BlockSpec((2 * MLSTM_HEADS, tb), lambda bi, si: (0, bi * (s // tb) + si)),
                  _const_spec(b_row.shape), _const_spec(norm_w.shape), _const_spec(sel.shape),
                  _const_spec(tri_t.shape)],
        out_specs=blk(MLSTM_WIDTH),
        scratch_shapes=[pltpu.VMEM((MLSTM_HEADS, HEAD_DIM, 2 * HEAD_DIM), F32),
                        pltpu.VMEM((MLSTM_HEADS, SUBLANES, LANES), F32),
                        pltpu.VMEM((2, 2 * MLSTM_HEADS, MLSTM_CHUNK), F32),
                        pltpu.VMEM((2, MLSTM_CHUNK, LANES), BF16)],
        compiler_params=pltpu.CompilerParams(dimension_semantics=("parallel", "arbitrary"),
                                             vmem_limit_bytes=VMEM_LIMIT_BYTES),
        name="mlstm",
    )(qm, km, vm, om, gates_t, b_row, norm_w, sel, tri_t)


def _mlstm_tri():
    idx = np.arange(MLSTM_CHUNK)
    return jnp.asarray(idx[:, None] <= idx[None, :], BF16)


def _mlstm_selector():
    r = np.arange(LANES)[:, None]
    c = np.arange(2 * MLSTM_HEADS * LANES)[None, :]
    return jnp.asarray((r < 3 * 2 * MLSTM_HEADS) & (r % (2 * MLSTM_HEADS) == c // LANES), BF16)


def _attn_kernel(q_ref, k_ref, v_ref, o_ref, s_ref, m_ref, acc_ref):
    t = ATTN_TILE
    nh = MLA_HEADS
    p = pl.program_id(1)
    half_tiles = pl.num_programs(1)
    diag = (lax.broadcasted_iota(jnp.int32, (t, t), 0) >= lax.broadcasted_iota(jnp.int32, (t, t), 1))
    neg_inf = jnp.float32(-jnp.inf)
    ones_blk = jnp.ones((t, HEAD_DIM), BF16)

    def reset():
        m_ref[...] = jnp.full(m_ref.shape, neg_inf, F32)
        acc_ref[...] = jnp.zeros_like(acc_ref)

    def scores(tile, j, slot, h):
        r0 = pl.multiple_of(j * t, t)
        q = q_ref[0, tile, :, h * QK_HEAD:(h + 1) * QK_HEAD]
        k = k_ref[0, pl.ds(r0, t), h * QK_HEAD:(h + 1) * QK_HEAD]
        s_ref[slot * nh + h] = lax.dot_general(q, k, (((1,), (1,)), ((), ())),
                                               preferred_element_type=F32)

    def accumulate(j, slot, h, masked):
        r0 = pl.multiple_of(j * t, t)
        v_aug = jnp.concatenate(
            [v_ref[0, pl.ds(r0, t), h * HEAD_DIM:(h + 1) * HEAD_DIM], ones_blk], axis=1)
        s = s_ref[slot * nh + h]
        if masked:
            s = jnp.where(diag, s, neg_inf)
        m_old = m_ref[h]
        m_new = jnp.maximum(m_old, jnp.max(s, axis=-1, keepdims=True))
        pr = jnp.exp2(s - jnp.concatenate([m_new] * (t // LANES), axis=1)).astype(BF16)
        alpha = jnp.exp2(m_old - m_new)
        acc_ref[h] = (jnp.concatenate([alpha, alpha], axis=1) * acc_ref[h]
                      + jnp.dot(pr, v_aug, preferred_element_type=F32))
        m_ref[h] = m_new

    def finish(tile, h):
        acc = acc_ref[h]
        o_ref[0, tile, :, h * HEAD_DIM:(h + 1) * HEAD_DIM] = (
            acc[:, :HEAD_DIM] / acc[:, HEAD_DIM:]).astype(BF16)

    def half_step(tile, j, slot_in, slot_out):
        for h in range(nh):
            scores(tile, j + 1, slot_out, h)
            accumulate(j, slot_in, h, masked=False)

    reset()
    for h in range(nh):
        scores(0, 0, 0, h)

    def body_a(i, carry):
        half_step(0, 2 * i, 0, 1)

        @pl.when(2 * i + 1 < p)
        def _():
            half_step(0, 2 * i + 1, 1, 0)
        return carry

    lax.fori_loop(0, (p + 1) // 2, body_a, 0)
    for parity in range(2):
        @pl.when(lax.rem(p, 2) == parity)
        def _(parity=parity):
            for h in range(nh):
                scores(1, 0, 2, h)
                accumulate(p, parity, h, masked=True)
                finish(0, h)

    reset()
    last = p + half_tiles
    half_step(1, 0, 2, 1)

    def body_b(i, carry):
        half_step(1, 2 * i + 1, 1, 0)

        @pl.when(2 * i + 2 < last)
        def _():
            half_step(1, 2 * i + 2, 0, 1)
        return carry

    lax.fori_loop(0, last // 2, body_b, 0)
    for h in range(nh):
        accumulate(last, lax.rem(last, 2), h, masked=True)
        finish(1, h)


def _attention(qa, ka, va):
    b, s, _ = qa.shape
    t = ATTN_TILE
    half = s // 2
    tile_pair = lambda width: pl.BlockSpec((1, 2, t, width), lambda bi, p: (bi, 0, p, 0))
    out = pl.pallas_call(
        _attn_kernel,
        out_shape=jax.ShapeDtypeStruct((b, 2, half, MLA_WIDTH), BF16),
        grid=(b, half // t),
        in_specs=[tile_pair(MLA_HEADS * QK_HEAD),
                  pl.BlockSpec((1, s, MLA_HEADS * QK_HEAD), lambda bi, p: (bi, 0, 0)),
                  pl.BlockSpec((1, s, MLA_WIDTH), lambda bi, p: (bi, 0, 0))],
        out_specs=tile_pair(MLA_WIDTH),
        scratch_shapes=[pltpu.VMEM((3 * MLA_HEADS, t, t), F32),
                        pltpu.VMEM((MLA_HEADS, t, LANES), F32),
                        pltpu.VMEM((MLA_HEADS, t, 2 * HEAD_DIM), F32)],
        compiler_params=pltpu.CompilerParams(dimension_semantics=("parallel", "arbitrary"),
                                             vmem_limit_bytes=VMEM_LIMIT_BYTES),
        name="mla_attention",
    )(qa.reshape(b, 2, half, MLA_HEADS * QK_HEAD), ka, va)
    return out.reshape(b, s, MLA_WIDTH)


def _shift_rows(u, tail, k):
    return pltpu.roll(jnp.concatenate([tail, u], axis=0), k, 0)[SUBLANES:]


def _mlp_tile(res_ref, xb_ref, wup_ref, cw_ref, cb_ref, wdn_ref, g_ref, b_ref, o_ref, tail_ref, h_ref, n_split):
    tm = FFN_ROW_TILE
    tf = FFN_COLS
    sub = FFN_SUB_ROWS
    assert CONV_WIDTH == 3 and CONV_WIDTH - 1 <= SUBLANES

    def conv_cols(lo, r0, tail):
        u = jnp.dot(xb_ref[r0:r0 + sub, :], wup_ref[:, lo:lo + tf], preferred_element_type=F32)
        uc = (_shift_rows(u, tail, 2) * cw_ref[0:1, lo:lo + tf]
              + _shift_rows(u, tail, 1) * cw_ref[1:2, lo:lo + tf]
              + u * cw_ref[2:3, lo:lo + tf] + cb_ref[:, lo:lo + tf])
        return uc, u[sub - SUBLANES:, :]

    for c in range(FFN_DIM // tf):
        glo, vlo = c * tf, FFN_DIM + c * tf
        gtail = tail_ref[:, glo:glo + tf]
        vtail = tail_ref[:, vlo:vlo + tf]
        for r0 in range(0, tm, sub):
            gate, gtail = conv_cols(glo, r0, gtail)
            val, vtail = conv_cols(vlo, r0, vtail)
            h_ref[r0:r0 + sub, c * tf:(c + 1) * tf] = ((gate + gate * jnp.tanh(gate)) * val).astype(BF16)
        tail_ref[:, glo:glo + tf] = gtail
        tail_ref[:, vlo:vlo + tf] = vtail

    blk = tm // n_split
    ys = [jnp.dot(h_ref[r * blk:(r + 1) * blk, :], wdn_ref[...], preferred_element_type=F32)
          for r in range(n_split)]
    for r in range(n_split):
        rs = slice(r * blk, (r + 1) * blk)
        o_ref[0, rs, :] = _layer_norm(DEEPNORM_ALPHA * res_ref[0, rs, :] + ys[r], g_ref[...], b_ref[...])


def _ffn_kernel(x_ref, wup_ref, cw_ref, cb_ref, wdn_ref, g_ref, b_ref, o_ref, tail_ref, h_ref, xb_ref):
    @pl.when(pl.program_id(1) == 0)
    def _():
        tail_ref[...] = jnp.zeros_like(tail_ref)

    xb_ref[...] = x_ref[0].astype(BF16)
    _mlp_tile(x_ref, xb_ref, wup_ref, cw_ref, cb_ref, wdn_ref, g_ref, b_ref, o_ref, tail_ref, h_ref,
              n_split=ROW_BLOCKS)


def _outproj_ffn_kernel(ym_ref, ya_ref, x_ref, wout_ref, gmix_ref, bmix_ref, wup_ref, cw_ref, cb_ref, wdn_ref,
                        g_ref, b_ref, o_ref, tail_ref, h_ref, xb_ref):
    tm = FFN_ROW_TILE

    @pl.when(pl.program_id(1) == 0)
    def _():
        tail_ref[...] = jnp.zeros_like(tail_ref)

    blk = tm // ROW_BLOCKS
    ys = []
    for r in range(ROW_BLOCKS):
        rs = slice(r * blk, (r + 1) * blk)
        ys.append(jnp.dot(ym_ref[0, rs, :], wout_ref[0:MLSTM_WIDTH, :], preferred_element_type=F32)
                  + jnp.dot(ya_ref[0, rs, :], wout_ref[MLSTM_WIDTH:, :], preferred_element_type=F32))
    for r in range(ROW_BLOCKS):
        rs = slice(r * blk, (r + 1) * blk)
        x1 = _layer_norm(DEEPNORM_ALPHA * x_ref[0, rs, :] + ys[r], gmix_ref[...], bmix_ref[...])
        o_ref[0, rs, :] = x1
        xb_ref[rs, :] = x1.astype(BF16)
    _mlp_tile(o_ref, xb_ref, wup_ref, cw_ref, cb_ref, wdn_ref, g_ref, b_ref, o_ref, tail_ref, h_ref, n_split=1)


def _layer_spec(shape, layer):
    nd = len(shape) - 1
    return pl.BlockSpec((None,) + tuple(shape[1:]), lambda *_: (layer,) + (0,) * nd,
                        pipeline_mode=pl.Buffered(1))


def _ffn_call(body, name, bsz, s, row_inputs, const_inputs, layer, stacked):
    tm = FFN_ROW_TILE
    blk = lambda width: pl.BlockSpec((1, tm, width), lambda bi, si: (bi, si, 0))
    return pl.pallas_call(
        body,
        out_shape=jax.ShapeDtypeStruct((bsz, s, D_MODEL), F32),
        grid=(bsz, s // tm),
        in_specs=([blk(a.shape[-1]) for a in row_inputs] + [_const_spec(a.shape) for a in const_inputs]
                  + [_layer_spec(a.shape, layer) for a in stacked]),
        out_specs=blk(D_MODEL),
        scratch_shapes=[pltpu.VMEM((SUBLANES, 2 * FFN_DIM), F32),
                        pltpu.VMEM((tm, FFN_DIM), BF16),
                        pltpu.VMEM((tm, D_MODEL), BF16)],
        compiler_params=pltpu.CompilerParams(dimension_semantics=("parallel", "arbitrary"),
                                             vmem_limit_bytes=VMEM_LIMIT_BYTES),
        name=name,
    )(*row_inputs, *const_inputs, *stacked)


def _ffn(x3, layer, stacked):
    bsz, s, _ = x3.shape
    return _ffn_call(_ffn_kernel, "conv_ffn_ln", bsz, s, (x3,), (), layer, stacked)


def _outproj_ffn(ym3, ya3, x3, w_out, g_mix, b_mix, layer, stacked):
    bsz, s, _ = x3.shape
    return _ffn_call(_outproj_ffn_kernel, "outproj_ffn_ln", bsz, s, (ym3, ya3, x3), (w_out, g_mix, b_mix),
                     layer, stacked)


def _pool_kernel(x_ref, pw_ref, ls_ref, g_ref, b_ref, o_ref, xbuf_ref):
    tm = POOL_ROW_TILE
    si = pl.program_id(1)

    @pl.when(si == 0)
    def _():
        xbuf_ref[0:POOL_HALO, :] = jnp.zeros((POOL_HALO, D_MODEL), F32)

    xbuf_ref[POOL_HALO:, :] = x_ref[0]
    sub = tm // ROW_BLOCKS
    ys = []
    for r in range(ROW_BLOCKS):
        r0 = r * sub
        t_pos = si * tm + r0 + lax.broadcasted_iota(jnp.int32, (sub, 1), 0)
        parts = []
        for gi, w in enumerate(POOL_WINDOWS):
            cols = slice(gi * POOL_GROUP_DIM, (gi + 1) * POOL_GROUP_DIM)
            ext = xbuf_ref[r0:r0 + POOL_HALO + sub, cols]
            k = 1
            while k < w:
                ext = ext + pltpu.roll(ext, k, 0)
                k *= 2
            cur = xbuf_ref[POOL_HALO + r0:POOL_HALO + r0 + sub, cols]
            cnt = jnp.minimum(t_pos + 1, w).astype(F32)
            pooled = (ext[POOL_HALO:, :] / cnt - cur).astype(BF16)
            parts.append(jnp.dot(pooled, pw_ref[gi], preferred_element_type=F32))
        ys.append(jnp.concatenate(parts, axis=1) * ls_ref[...])
    for r in range(ROW_BLOCKS):
        rs = slice(r * sub, (r + 1) * sub)
        o_ref[0, rs, :] = _layer_norm(DEEPNORM_ALPHA * x_ref[0, rs, :] + ys[r], g_ref[...], b_ref[...])
    xbuf_ref[0:POOL_HALO, :] = x_ref[0, tm - POOL_HALO:, :]


def _pool(x3, pool_w, layer_scale, g, b):
    bsz, s, _ = x3.shape
    tm = POOL_ROW_TILE
    blk = pl.BlockSpec((1, tm, D_MODEL), lambda bi, si: (bi, si, 0))
    return pl.pallas_call(
        _pool_kernel,
        out_shape=jax.ShapeDtypeStruct(x3.shape, F32),
        grid=(bsz, s // tm),
        in_specs=[blk, _const_spec(pool_w.shape), _const_spec(layer_scale.shape),
                  _const_spec(g.shape), _const_spec(b.shape)],
        out_specs=blk,
        scratch_shapes=[pltpu.VMEM((POOL_HALO + tm, D_MODEL), F32)],
        compiler_params=pltpu.CompilerParams(dimension_semantics=("parallel", "arbitrary"),
                                             vmem_limit_bytes=VMEM_LIMIT_BYTES),
        name="pool_ln",
    )(x3, pool_w, layer_scale, g, b)


def _rope_block_cols(w_rope):
    z = lambda n: jnp.zeros(w_rope.shape[:-1] + (n,), w_rope.dtype)
    return jnp.concatenate([z(ROPE_LO), w_rope[..., :ROPE_HALF], z(ROPE_HI - ROPE_LO - ROPE_HALF),
                            w_rope[..., ROPE_HALF:], z(LANES - ROPE_HI - ROPE_HALF)], axis=-1)


def _prep_w_in(w_in):
    offs = np.cumsum((0,) + IN_SIZES)
    parts = [w_in[:, offs[i]:offs[i + 1]] for i in range(len(IN_SIZES))]
    q_m, k_m, v_m, o_m, i_g, f_g, c_q, c_kv, k_r = parts
    gk = _rope_block_cols(k_r).at[:, 0:2 * MLSTM_HEADS].set(jnp.concatenate([i_g, f_g], axis=1))
    return jnp.concatenate([q_m, k_m, v_m, o_m, c_q, c_kv, gk], axis=1).astype(BF16)


def _prep_w_uq(w_uq):
    w = w_uq.reshape(Q_LORA, MLA_HEADS, HEAD_DIM + ROPE_DIM)
    blk = jnp.concatenate([w[..., :HEAD_DIM], _rope_block_cols(w[..., HEAD_DIM:])], axis=-1)
    return blk.reshape(Q_LORA, MLA_HEADS * QK_HEAD).astype(BF16)


def _prep_w_ukv(w_ukv):
    w = w_ukv.reshape(KV_LORA, MLA_HEADS, 2 * HEAD_DIM)
    k_nope = w[..., :HEAD_DIM].reshape(KV_LORA, MLA_WIDTH)
    v = w[..., HEAD_DIM:].reshape(KV_LORA, MLA_WIDTH)
    return jnp.concatenate([k_nope, v], axis=1).astype(BF16)


def _rope_tables():
    inv_freq = ROPE_THETA ** (-jnp.arange(0, ROPE_DIM, 2, dtype=F32) / ROPE_DIM)
    n_blk = ROW_TILE // LANES
    freq = jnp.tile(inv_freq, n_blk)[None, :]
    selc = np.zeros((n_blk, 2 * LANES, LANES), np.float32)
    sels = np.zeros((n_blk, 2 * LANES, LANES), np.float32)
    f = np.arange(ROPE_HALF)
    for k in range(n_blk):
        for half in range(2):
            rows = half * LANES + ROPE_HALF * k + f
            selc[k, rows, ROPE_LO + f] = 1.0
            selc[k, rows, ROPE_HI + f] = 1.0
            sels[k, rows, ROPE_LO + f] = -1.0
            sels[k, rows, ROPE_HI + f] = 1.0
    return freq, jnp.asarray(selc, BF16), jnp.asarray(sels, BF16)


def _pack_positions(positions):
    n_blk = ROW_TILE // LANES
    pos = positions.astype(F32).reshape(-1, n_blk, LANES).transpose(0, 2, 1)
    return jnp.repeat(pos, LANES // n_blk, axis=-1).reshape(-1, LANES)


def kernel(x, positions, even_w_in, even_b_igate, even_b_fgate, even_mlstm_norm, even_q_norm,
           even_kv_norm, even_w_uq, even_w_ukv, even_w_out, odd_pool_w, odd_layer_scale,
           ffn_w_up, ffn_conv_w, ffn_conv_b, ffn_w_down, ln_mix_g, ln_mix_b, ln_ffn_g, ln_ffn_b):
    bsz, s, d = x.shape
    t = bsz * s
    row = lambda v: v.reshape(1, -1)

    freq, selc, sels = _rope_tables()
    qm, km, vm, om, gates, qa, ka, va = _inproj(
        x.reshape(t, d), _pack_positions(positions), _prep_w_in(even_w_in[0]),
        row(even_q_norm[0]), row(even_kv_norm[0]), _prep_w_uq(even_w_uq[0]), _prep_w_ukv(even_w_ukv[0]),
        freq, selc, sels)

    gate_bias = jnp.concatenate([even_b_igate[0], even_b_fgate[0]])
    b3 = lambda a: a.reshape(bsz, s, a.shape[-1])
    ym = _mlstm(b3(qm), b3(km), b3(vm), b3(om), gates, gate_bias.reshape(-1, 1),
                row(even_mlstm_norm[0]), _mlstm_selector(), _mlstm_tri())
    ya = _attention(b3(qa), b3(ka), b3(va))

    stack_rows = lambda v: v.reshape(DEPTH, 1, -1)
    gate_half = jnp.concatenate([jnp.full((FFN_DIM,), 0.5, F32), jnp.ones((FFN_DIM,), F32)])
    ffn_params = (ffn_w_up.astype(BF16), ffn_conv_w * gate_half, stack_rows(ffn_conv_b * gate_half),
                  ffn_w_down.astype(BF16), stack_rows(ln_ffn_g), stack_rows(ln_ffn_b))
    x1 = _outproj_ffn(ym, ya, x, even_w_out[0].astype(BF16), row(ln_mix_g[0]), row(ln_mix_b[0]), 0, ffn_params)

    x2 = _pool(x1, odd_pool_w[0].astype(BF16), row(odd_layer_scale[0]), row(ln_mix_g[1]), row(ln_mix_b[1]))
    return _ffn(x2, 1, ffn_params)
```

```python
import jax
import jax.numpy as jnp
import numpy as np
from jax import lax
from jax.experimental import pallas as pl
from jax.experimental.pallas import tpu as pltpu

F32 = jnp.float32
BF16 = jnp.bfloat16

D_MODEL = 1024
DEPTH = 2
MLSTM_HEADS = 4
HEAD_DIM = 128
MLSTM_WIDTH = MLSTM_HEADS * HEAD_DIM
MLA_HEADS = 4
ROPE_DIM = 64
ROPE_HALF = ROPE_DIM // 2
Q_LORA = 256
KV_LORA = 128
MLA_WIDTH = MLA_HEADS * HEAD_DIM
ROPE_THETA = 10000.0
POOL_WINDOWS = (2, 4, 8, 16)
POOL_GROUP_DIM = D_MODEL // len(POOL_WINDOWS)
FFN_DIM = 2816
CONV_WIDTH = 3
LN_EPS = 1e-5
RMS_EPS = 1e-6
DEEPNORM_ALPHA = (2 * DEPTH) ** 0.25
IN_SIZES = (MLSTM_WIDTH, MLSTM_WIDTH, MLSTM_WIDTH, MLSTM_WIDTH, MLSTM_HEADS, MLSTM_HEADS,
            Q_LORA, KV_LORA, ROPE_DIM)

LANES = 128
SUBLANES = 8
VMEM_LIMIT_BYTES = 56 * 1024 * 1024

ROPE_LO = 8
ROPE_HI = ROPE_LO + LANES // 2
QK_HEAD = 2 * LANES

ROW_TILE = 512
MLSTM_CHUNK = 256
MLSTM_BLOCK = 2048
MLSTM_GROUP = 4
ATTN_TILE = 512
FFN_COLS = 256
FFN_ROW_TILE = 1024
FFN_SUB_ROWS = 128
POOL_ROW_TILE = 1024
ROW_BLOCKS = 4
POOL_HALO = 16


def _const_spec(shape):
    nd = len(shape)
    return pl.BlockSpec(shape, lambda *_: (0,) * nd, pipeline_mode=pl.Buffered(1))


def _layer_norm(z, g, b):
    mu = jnp.mean(z, axis=-1, keepdims=True)
    d = z - mu
    var = jnp.mean(d * d, axis=-1, keepdims=True)
    return d * lax.rsqrt(var + LN_EPS) * g + b


def _rms_norm(z, g):
    return z * lax.rsqrt(jnp.mean(z * z, axis=-1, keepdims=True) + RMS_EPS) * g


def _log_sigmoid(z):
    return -(jnp.maximum(-z, 0.0) + jnp.log1p(jnp.exp(-jnp.abs(z))))


def _split3(z):
    hi = z.astype(BF16)
    r1 = z - hi.astype(F32)
    mid = r1.astype(BF16)
    lo = (r1 - mid.astype(F32)).astype(BF16)
    return hi, mid, lo


def _split2_lanes(z):
    hi = z.astype(BF16)
    return jnp.concatenate([hi, (z - hi.astype(F32)).astype(BF16)], axis=1)


def _inproj_kernel(x_ref, pos_ref, w_in_ref, qn_ref, kvn_ref, wuq_ref, wukv_ref, freq_ref, selc_ref,
                   sels_ref, qm_ref, km_ref, vm_ref, om_ref, gates_ref, qa_ref, ka_ref, va_ref):
    ang = pos_ref[...] * freq_ref[...]
    cos2 = _split2_lanes(jnp.cos(ang))
    sin2 = _split2_lanes(jnp.sin(ang))
    n_blk = ROW_TILE // LANES
    cosm = jnp.concatenate([jnp.dot(cos2, selc_ref[k], preferred_element_type=F32) for k in range(n_blk)], axis=0)
    sinm = jnp.concatenate([jnp.dot(sin2, sels_ref[k], preferred_element_type=F32) for k in range(n_blk)], axis=0)

    def rope(blk):
        return blk * cosm + pltpu.roll(blk, LANES // 2, 1) * sinm

    xb = x_ref[...].astype(BF16)

    def proj(lo, width):
        return jnp.dot(xb, w_in_ref[:, lo:lo + width], preferred_element_type=F32)

    w = MLSTM_WIDTH
    c_q = proj(4 * w, Q_LORA)
    ckv_gk = proj(4 * w + Q_LORA, KV_LORA + LANES)
    c_kv = ckv_gk[:, :KV_LORA]
    gk = ckv_gk[:, KV_LORA:]
    gates_ref[...] = gk.T[0:2 * MLSTM_HEADS, :]
    k_rope = rope(gk).astype(BF16)

    qm_ref[...] = proj(0, w).astype(BF16)
    km_ref[...] = (proj(w, w) * (HEAD_DIM ** -0.5)).astype(BF16)
    vm_ref[...] = proj(2 * w, w).astype(BF16)
    om_ref[...] = proj(3 * w, w)

    scale = (HEAD_DIM + ROPE_DIM) ** -0.5 * float(np.log2(np.e))
    q = jnp.dot(_rms_norm(c_q, qn_ref[...]).astype(BF16), wuq_ref[...], preferred_element_type=F32)
    kv = jnp.dot(_rms_norm(c_kv, kvn_ref[...]).astype(BF16), wukv_ref[...], preferred_element_type=F32)
    for h in range(MLA_HEADS):
        lo = h * QK_HEAD
        qa_ref[:, lo:lo + LANES] = (q[:, lo:lo + LANES] * scale).astype(BF16)
        qa_ref[:, lo + LANES:lo + QK_HEAD] = (rope(q[:, lo + LANES:lo + QK_HEAD]) * scale).astype(BF16)
        ka_ref[:, lo:lo + LANES] = kv[:, h * HEAD_DIM:(h + 1) * HEAD_DIM].astype(BF16)
        ka_ref[:, lo + LANES:lo + QK_HEAD] = k_rope
    va_ref[...] = kv[:, MLA_WIDTH:].astype(BF16)


def _inproj(x2, pos_packed, w_in_p, q_norm, kv_norm, wuq_p, wukv_p, freq, selc, sels):
    t = x2.shape[0]
    tm = ROW_TILE
    row = lambda width: pl.BlockSpec((tm, width), lambda i: (i, 0))
    out_shapes = (
        jax.ShapeDtypeStruct((t, MLSTM_WIDTH), BF16),
        jax.ShapeDtypeStruct((t, MLSTM_WIDTH), BF16),
        jax.ShapeDtypeStruct((t, MLSTM_WIDTH), BF16),
        jax.ShapeDtypeStruct((t, MLSTM_WIDTH), F32),
        jax.ShapeDtypeStruct((2 * MLSTM_HEADS, t), F32),
        jax.ShapeDtypeStruct((t, MLA_HEADS * QK_HEAD), BF16),
        jax.ShapeDtypeStruct((t, MLA_HEADS * QK_HEAD), BF16),
        jax.ShapeDtypeStruct((t, MLA_WIDTH), BF16),
    )
    return pl.pallas_call(
        _inproj_kernel,
        out_shape=out_shapes,
        grid=(t // tm,),
        in_specs=[row(D_MODEL), pl.BlockSpec((tm // 4, LANES), lambda i: (i, 0)),
                  _const_spec(w_in_p.shape), _const_spec(q_norm.shape),
                  _const_spec(kv_norm.shape), _const_spec(wuq_p.shape), _const_spec(wukv_p.shape),
                  _const_spec(freq.shape), _const_spec(selc.shape), _const_spec(sels.shape)],
        out_specs=tuple(pl.BlockSpec((2 * MLSTM_HEADS, tm), lambda i: (0, i)) if s.shape[0] != t else row(s.shape[1])
                        for s in out_shapes),
        compiler_params=pltpu.CompilerParams(dimension_semantics=("parallel",),
                                             vmem_limit_bytes=VMEM_LIMIT_BYTES),
        name="inproj",
    )(x2, pos_packed, w_in_p, q_norm, kv_norm, wuq_p, wukv_p, freq, selc, sels)


def _mlstm_kernel(q_ref, k_ref, v_ref, o_ref, gt_ref, brow_ref, nw_ref, sel_ref, tri_ref, y_ref, ct_ref, m_ref,
                  a8_ref, xt_ref):
    L = MLSTM_CHUNK
    d = HEAD_DIM
    nh = MLSTM_HEADS
    heads = range(nh)

    @pl.when(pl.program_id(1) == 0)
    def _():
        ct_ref[...] = jnp.zeros_like(ct_ref)
        m_ref[...] = jnp.zeros_like(m_ref)

    causal = lax.broadcasted_iota(jnp.int32, (L, L), 0) >= lax.broadcasted_iota(jnp.int32, (L, L), 1)
    top_rows = lax.broadcasted_iota(jnp.int32, (2 * nh, L), 0) < nh
    ones_blk = jnp.ones((L, d), BF16)
    neg_inf = jnp.float32(-jnp.inf)

    def hs(h):
        return slice(h * d, (h + 1) * d)

    def lanes(j):
        return slice(j * LANES, (j + 1) * LANES)

    def scalar_part(c, slot):
        g8 = gt_ref[:, pl.ds(pl.multiple_of(c * L, L), L)] + brow_ref[...]
        b8 = sum(jnp.dot(p, tri_ref[...], preferred_element_type=F32) for p in _split3(_log_sigmoid(g8)))
        a8 = g8 - pltpu.roll(b8, nh, 0)
        w8 = jnp.exp(a8 - jnp.max(a8, axis=-1, keepdims=True))
        r = jnp.where(top_rows, w8, b8)
        rpad = jnp.concatenate([p.astype(F32) for p in _split3(r)]
                               + [jnp.zeros((LANES - 3 * 2 * nh, L), F32)], axis=0)
        a8_ref[slot] = a8
        xt_ref[slot] = rpad.T.astype(BF16)

    def local_part(c, slot):
        rows = pl.ds(pl.multiple_of(c * L, L), L)
        a8 = a8_ref[slot]
        bc = jnp.dot(xt_ref[slot], sel_ref[...], preferred_element_type=F32)
        q = [q_ref[0, rows, hs(h)] for h in heads]
        k = [k_ref[0, rows, hs(h)] for h in heads]
        s = [lax.dot_general(q[h], k[h], (((1,), (1,)), ((), ())), preferred_element_type=F32) for h in heads]
        a_low = [jnp.where(causal, a8[h:h + 1, :], neg_inf) for h in heads]
        mp = [jnp.max(a_low[h], axis=-1, keepdims=True) for h in heads]
        a_mat = [(s[h] * jnp.exp(a_low[h] - mp[h])).astype(BF16) for h in heads]
        v_aug = [jnp.concatenate([v_ref[0, rows, hs(h)], ones_blk], axis=1) for h in heads]
        intra = [jnp.dot(a_mat[h], v_aug[h], preferred_element_type=F32) for h in heads]
        w_rep = [bc[:, lanes(h)] for h in heads]
        b_rep = [bc[:, lanes(nh + h)] for h in heads]
        wv = [jnp.concatenate([(v_aug[h][:, :d].astype(F32) * w_rep[h]).astype(BF16), w_rep[h].astype(BF16)], axis=1)
              for h in heads]
        upd = [lax.dot_general(k[h], wv[h], (((0,), (0,)), ((), ())), preferred_element_type=F32) for h in heads]
        return rows, q, mp, b_rep, intra, upd

    def carried_part(local):
        rows, q, mp, b_rep, intra, upd = local
        ct = [ct_ref[h] for h in heads]
        inter = [jnp.dot(q[h], ct[h].astype(BF16), preferred_element_type=F32) for h in heads]
        m_prev = [m_ref[h][0:1, :] for h in heads]
        for h in heads:
            mp_rep = jnp.broadcast_to(mp[h], (L, LANES))
            m_rep = jnp.maximum(mp_rep, m_prev[h])
            e_intra = jnp.exp(mp_rep - m_rep)
            e_inter = jnp.exp(m_prev[h] - m_rep)
            num = e_intra * intra[h][:, :d] + e_inter * inter[h][:, :d]
            den = e_intra * intra[h][:, d:] + e_inter * inter[h][:, d:]
            hraw = num / jnp.maximum(jnp.abs(den), jnp.exp(-(b_rep[h] + m_rep)))
            gate = jax.nn.sigmoid(o_ref[0, rows, hs(h)])
            y_ref[0, rows, hs(h)] = (_rms_norm(hraw, nw_ref[:, hs(h)]) * gate).astype(BF16)
            mp_last = mp_rep[L - 1:L, :]
            m_last = jnp.maximum(mp_last, m_prev[h])
            keep = jnp.exp(m_prev[h] - m_last)
            add = jnp.exp(mp_last - m_last)
            ct_ref[h] = (jnp.concatenate([keep, keep], axis=1) * ct[h]
                         + jnp.concatenate([add, add], axis=1) * upd[h])
            m_ref[h] = jnp.broadcast_to(b_rep[h][L - 1:L, :] + m_last, (SUBLANES, LANES))

    group = MLSTM_GROUP
    n_groups = MLSTM_BLOCK // (group * L)
    for g in range(group):
        scalar_part(g, g)

    def step(i, carry):
        local = [local_part(group * i + g, g) for g in range(group)]
        for g in range(group):
            carried_part(local[g])
        nxt = jnp.minimum(i + 1, n_groups - 1)
        for g in range(group):
            scalar_part(group * nxt + g, g)
        return carry

    lax.fori_loop(0, n_groups, step, 0)


def _mlstm(qm, km, vm, om, gates_t, b_row, norm_w, sel, tri_t):
    b, s, _ = qm.shape
    tb = MLSTM_BLOCK
    blk = lambda width: pl.BlockSpec((1, tb, width), lambda bi, si: (bi, si, 0))
    return pl.pallas_call(
        _mlstm_kernel,
        out_shape=jax.ShapeDtypeStruct((b, s, MLSTM_WIDTH), BF16),
        grid=(b, s // tb),
        in_specs=[blk(MLSTM_WIDTH), blk(MLSTM_WIDTH), blk(MLSTM_WIDTH), blk(MLSTM_WIDTH),
                  pl.BlockSpec((2 * MLSTM_HEADS, tb), lambda bi, si: (0, bi * (s // tb) + si)),
                  _const_spec(b_row.shape), _const_spec(norm_w.shape), _const_spec(sel.shape),
                  _const_spec(tri_t.shape)],
        out_specs=blk(MLSTM_WIDTH),
        scratch_shapes=[pltpu.VMEM((MLSTM_HEADS, HEAD_DIM, 2 * HEAD_DIM), F32),
                        pltpu.VMEM((MLSTM_HEADS, SUBLANES, LANES), F32),
                        pltpu.VMEM((MLSTM_GROUP, 2 * MLSTM_HEADS, MLSTM_CHUNK), F32),
                        pltpu.VMEM((MLSTM_GROUP, MLSTM_CHUNK, LANES), BF16)],
        compiler_params=pltpu.CompilerParams(dimension_semantics=("parallel", "arbitrary"),
                                             vmem_limit_bytes=VMEM_LIMIT_BYTES),
        name="mlstm",
    )(qm, km, vm, om, gates_t, b_row, norm_w, sel, tri_t)


def _mlstm_tri():
    idx = np.arange(MLSTM_CHUNK)
    return jnp.asarray(idx[:, None] <= idx[None, :], BF16)


def _mlstm_selector():
    r = np.arange(LANES)[:, None]
    c = np.arange(2 * MLSTM_HEADS * LANES)[None, :]
    return jnp.asarray((r < 3 * 2 * MLSTM_HEADS) & (r % (2 * MLSTM_HEADS) == c // LANES), BF16)


def _attn_kernel(q_ref, k_ref, v_ref, o_ref, s_ref, m_ref, acc_ref):
    t = ATTN_TILE
    nh = MLA_HEADS
    p = pl.program_id(1)
    half_tiles = pl.num_programs(1)
    diag = (lax.broadcasted_iota(jnp.int32, (t, t), 0) >= lax.broadcasted_iota(jnp.int32, (t, t), 1))
    neg_inf = jnp.float32(-jnp.inf)
    ones_blk = jnp.ones((t, HEAD_DIM), BF16)

    def reset():
        m_ref[...] = jnp.full(m_ref.shape, neg_inf, F32)
        acc_ref[...] = jnp.zeros_like(acc_ref)

    def scores(tile, j, slot, h):
        r0 = pl.multiple_of(j * t, t)
        q = q_ref[0, tile, :, h * QK_HEAD:(h + 1) * QK_HEAD]
        k = k_ref[0, pl.ds(r0, t), h * QK_HEAD:(h + 1) * QK_HEAD]
        s_ref[slot * nh + h] = lax.dot_general(q, k, (((1,), (1,)), ((), ())),
                                               preferred_element_type=F32)

    def accumulate(j, slot, h, masked):
        r0 = pl.multiple_of(j * t, t)
        v_aug = jnp.concatenate(
            [v_ref[0, pl.ds(r0, t), h * HEAD_DIM:(h + 1) * HEAD_DIM], ones_blk], axis=1)
        s = s_ref[slot * nh + h]
        if masked:
            s = jnp.where(diag, s, neg_inf)
        m_old = m_ref[h]
        m_new = jnp.maximum(m_old, jnp.max(s, axis=-1, keepdims=True))
        pr = jnp.exp2(s - jnp.concatenate([m_new] * (t // LANES), axis=1)).astype(BF16)
        alpha = jnp.exp2(m_old - m_new)
        acc_ref[h] = (jnp.concatenate([alpha, alpha], axis=1) * acc_ref[h]
                      + jnp.dot(pr, v_aug, preferred_element_type=F32))
        m_ref[h] = m_new

    def finish(tile, h):
        acc = acc_ref[h]
        o_ref[0, tile, :, h * HEAD_DIM:(h + 1) * HEAD_DIM] = (
            acc[:, :HEAD_DIM] / acc[:, HEAD_DIM:]).astype(BF16)

    def half_step(tile, j, slot_in, slot_out):
        for h in range(nh):
            scores(tile, j + 1, slot_out, h)
            accumulate(j, slot_in, h, masked=False)

    reset()
    for h in range(nh):
        scores(0, 0, 0, h)

    def body_a(i, carry):
        half_step(0, 2 * i, 0, 1)

        @pl.when(2 * i + 1 < p)
        def _():
            half_step(0, 2 * i + 1, 1, 0)
        return carry

    lax.fori_loop(0, (p + 1) // 2, body_a, 0)
    for parity in range(2):
        @pl.when(lax.rem(p, 2) == parity)
        def _(parity=parity):
            for h in range(nh):
                scores(1, 0, 2, h)
                accumulate(p, parity, h, masked=True)
                finish(0, h)

    reset()
    last = p + half_tiles
    half_step(1, 0, 2, 1)

    def body_b(i, carry):
        half_step(1, 2 * i + 1, 1, 0)

        @pl.when(2 * i + 2 < last)
        def _():
            half_step(1, 2 * i + 2, 0, 1)
        return carry

    lax.fori_loop(0, last // 2, body_b, 0)
    for h in range(nh):
        accumulate(last, lax.rem(last, 2), h, masked=True)
        finish(1, h)


def _attention(qa, ka, va):
    b, s, _ = qa.shape
    t = ATTN_TILE
    half = s // 2
    tile_pair = lambda width: pl.BlockSpec((1, 2, t, width), lambda bi, p: (bi, 0, p, 0))
    out = pl.pallas_call(
        _attn_kernel,
        out_shape=jax.ShapeDtypeStruct((b, 2, half, MLA_WIDTH), BF16),
        grid=(b, half // t),
        in_specs=[tile_pair(MLA_HEADS * QK_HEAD),
                  pl.BlockSpec((1, s, MLA_HEADS * QK_HEAD), lambda bi, p: (bi, 0, 0)),
                  pl.BlockSpec((1, s, MLA_WIDTH), lambda bi, p: (bi, 0, 0))],
        out_specs=tile_pair(MLA_WIDTH),
        scratch_shapes=[pltpu.VMEM((3 * MLA_HEADS, t, t), F32),
                        pltpu.VMEM((MLA_HEADS, t, LANES), F32),
                        pltpu.VMEM((MLA_HEADS, t, 2 * HEAD_DIM), F32)],
        compiler_params=pltpu.CompilerParams(dimension_semantics=("parallel", "arbitrary"),
                                             vmem_limit_bytes=VMEM_LIMIT_BYTES),
        name="mla_attention",
    )(qa.reshape(b, 2, half, MLA_HEADS * QK_HEAD), ka, va)
    return out.reshape(b, s, MLA_WIDTH)


def _shift_rows(u, tail, k):
    return pltpu.roll(jnp.concatenate([tail, u], axis=0), k, 0)[SUBLANES:]


def _mlp_tile(res_ref, xb_ref, wup_ref, cw_ref, cb_ref, wdn_ref, g_ref, b_ref, o_ref, tail_ref, h_ref, n_split):
    tm = FFN_ROW_TILE
    tf = FFN_COLS
    sub = FFN_SUB_ROWS
    assert CONV_WIDTH == 3 and CONV_WIDTH - 1 <= SUBLANES

    def conv_cols(lo, r0, tail):
        u = jnp.dot(xb_ref[r0:r0 + sub, :], wup_ref[:, lo:lo + tf], preferred_element_type=F32)
        uc = (_shift_rows(u, tail, 2) * cw_ref[0:1, lo:lo + tf]
              + _shift_rows(u, tail, 1) * cw_ref[1:2, lo:lo + tf]
              + u * cw_ref[2:3, lo:lo + tf] + cb_ref[:, lo:lo + tf])
        return uc, u[sub - SUBLANES:, :]

    for c in range(FFN_DIM // tf):
        glo, vlo = c * tf, FFN_DIM + c * tf
        gtail = tail_ref[:, glo:glo + tf]
        vtail = tail_ref[:, vlo:vlo + tf]
        for r0 in range(0, tm, sub):
            gate, gtail = conv_cols(glo, r0, gtail)
            val, vtail = conv_cols(vlo, r0, vtail)
            h_ref[r0:r0 + sub, c * tf:(c + 1) * tf] = ((gate + gate * jnp.tanh(gate)) * val).astype(BF16)
        tail_ref[:, glo:glo + tf] = gtail
        tail_ref[:, vlo:vlo + tf] = vtail

    blk = tm // n_split
    ys = [jnp.dot(h_ref[r * blk:(r + 1) * blk, :], wdn_ref[...], preferred_element_type=F32)
          for r in range(n_split)]
    for r in range(n_split):
        rs = slice(r * blk, (r + 1) * blk)
        o_ref[0, rs, :] = _layer_norm(DEEPNORM_ALPHA * res_ref[0, rs, :] + ys[r], g_ref[...], b_ref[...])


def _ffn_kernel(x_ref, wup_ref, cw_ref, cb_ref, wdn_ref, g_ref, b_ref, o_ref, tail_ref, h_ref, xb_ref):
    @pl.when(pl.program_id(1) == 0)
    def _():
        tail_ref[...] = jnp.zeros_like(tail_ref)

    xb_ref[...] = x_ref[0].astype(BF16)
    _mlp_tile(x_ref, xb_ref, wup_ref, cw_ref, cb_ref, wdn_ref, g_ref, b_ref, o_ref, tail_ref, h_ref,
              n_split=ROW_BLOCKS)


def _outproj_ffn_kernel(ym_ref, ya_ref, x_ref, wout_ref, gmix_ref, bmix_ref, wup_ref, cw_ref, cb_ref, wdn_ref,
                        g_ref, b_ref, o_ref, tail_ref, h_ref, xb_ref):
    tm = FFN_ROW_TILE

    @pl.when(pl.program_id(1) == 0)
    def _():
        tail_ref[...] = jnp.zeros_like(tail_ref)

    blk = tm // ROW_BLOCKS
    ys = []
    for r in range(ROW_BLOCKS):
        rs = slice(r * blk, (r + 1) * blk)
        ys.append(jnp.dot(ym_ref[0, rs, :], wout_ref[0:MLSTM_WIDTH, :], preferred_element_type=F32)
                  + jnp.dot(ya_ref[0, rs, :], wout_ref[MLSTM_WIDTH:, :], preferred_element_type=F32))
    for r in range(ROW_BLOCKS):
        rs = slice(r * blk, (r + 1) * blk)
        x1 = _layer_norm(DEEPNORM_ALPHA * x_ref[0, rs, :] + ys[r], gmix_ref[...], bmix_ref[...])
        o_ref[0, rs, :] = x1
        xb_ref[rs, :] = x1.astype(BF16)
    _mlp_tile(o_ref, xb_ref, wup_ref, cw_ref, cb_ref, wdn_ref, g_ref, b_ref, o_ref, tail_ref, h_ref, n_split=1)


def _layer_spec(shape, layer):
    nd = len(shape) - 1
    return pl.BlockSpec((None,) + tuple(shape[1:]), lambda *_: (layer,) + (0,) * nd,
                        pipeline_mode=pl.Buffered(1))


def _ffn_call(body, name, bsz, s, row_inputs, const_inputs, layer, stacked):
    tm = FFN_ROW_TILE
    blk = lambda width: pl.BlockSpec((1, tm, width), lambda bi, si: (bi, si, 0))
    return pl.pallas_call(
        body,
        out_shape=jax.ShapeDtypeStruct((bsz, s, D_MODEL), F32),
        grid=(bsz, s // tm),
        in_specs=([blk(a.shape[-1]) for a in row_inputs] + [_const_spec(a.shape) for a in const_inputs]
                  + [_layer_spec(a.shape, layer) for a in stacked]),
        out_specs=blk(D_MODEL),
        scratch_shapes=[pltpu.VMEM((SUBLANES, 2 * FFN_DIM), F32),
                        pltpu.VMEM((tm, FFN_DIM), BF16),
                        pltpu.VMEM((tm, D_MODEL), BF16)],
        compiler_params=pltpu.CompilerParams(dimension_semantics=("parallel", "arbitrary"),
                                             vmem_limit_bytes=VMEM_LIMIT_BYTES),
        name=name,
    )(*row_inputs, *const_inputs, *stacked)


def _ffn(x3, layer, stacked):
    bsz, s, _ = x3.shape
    return _ffn_call(_ffn_kernel, "conv_ffn_ln", bsz, s, (x3,), (), layer, stacked)


def _outproj_ffn(ym3, ya3, x3, w_out, g_mix, b_mix, layer, stacked):
    bsz, s, _ = x3.shape
    return _ffn_call(_outproj_ffn_kernel, "outproj_ffn_ln", bsz, s, (ym3, ya3, x3), (w_out, g_mix, b_mix),
                     layer, stacked)


def _pool_kernel(x_ref, pw_ref, ls_ref, g_ref, b_ref, o_ref, xbuf_ref):
    tm = POOL_ROW_TILE
    si = pl.program_id(1)

    @pl.when(si == 0)
    def _():
        xbuf_ref[0:POOL_HALO, :] = jnp.zeros((POOL_HALO, D_MODEL), F32)

    xbuf_ref[POOL_HALO:, :] = x_ref[0]
    sub = tm // ROW_BLOCKS
    ys = []
    for r in range(ROW_BLOCKS):
        r0 = r * sub
        t_pos = si * tm + r0 + lax.broadcasted_iota(jnp.int32, (sub, 1), 0)
        parts = []
        for gi, w in enumerate(POOL_WINDOWS):
            cols = slice(gi * POOL_GROUP_DIM, (gi + 1) * POOL_GROUP_DIM)
            ext = xbuf_ref[r0:r0 + POOL_HALO + sub, cols]
            k = 1
            while k < w:
                ext = ext + pltpu.roll(ext, k, 0)
                k *= 2
            cur = xbuf_ref[POOL_HALO + r0:POOL_HALO + r0 + sub, cols]
            cnt = jnp.minimum(t_pos + 1, w).astype(F32)
            pooled = (ext[POOL_HALO:, :] / cnt - cur).astype(BF16)
            parts.append(jnp.dot(pooled, pw_ref[gi], preferred_element_type=F32))
        ys.append(jnp.concatenate(parts, axis=1) * ls_ref[...])
    for r in range(ROW_BLOCKS):
        rs = slice(r * sub, (r + 1) * sub)
        o_ref[0, rs, :] = _layer_norm(DEEPNORM_ALPHA * x_ref[0, rs, :] + ys[r], g_ref[...], b_ref[...])
    xbuf_ref[0:POOL_HALO, :] = x_ref[0, tm - POOL_HALO:, :]


def _pool(x3, pool_w, layer_scale, g, b):
    bsz, s, _ = x3.shape
    tm = POOL_ROW_TILE
    blk = pl.BlockSpec((1, tm, D_MODEL), lambda bi, si: (bi, si, 0))
    return pl.pallas_call(
        _pool_kernel,
        out_shape=jax.ShapeDtypeStruct(x3.shape, F32),
        grid=(bsz, s // tm),
        in_specs=[blk, _const_spec(pool_w.shape), _const_spec(layer_scale.shape),
                  _const_spec(g.shape), _const_spec(b.shape)],
        out_specs=blk,
        scratch_shapes=[pltpu.VMEM((POOL_HALO + tm, D_MODEL), F32)],
        compiler_params=pltpu.CompilerParams(dimension_semantics=("parallel", "arbitrary"),
                                             vmem_limit_bytes=VMEM_LIMIT_BYTES),
        name="pool_ln",
    )(x3, pool_w, layer_scale, g, b)


def _rope_block_cols(w_rope):
    z = lambda n: jnp.zeros(w_rope.shape[:-1] + (n,), w_rope.dtype)
    return jnp.concatenate([z(ROPE_LO), w_rope[..., :ROPE_HALF], z(ROPE_HI - ROPE_LO - ROPE_HALF),
                            w_rope[..., ROPE_HALF:], z(LANES - ROPE_HI - ROPE_HALF)], axis=-1)


def _prep_w_in(w_in):
    offs = np.cumsum((0,) + IN_SIZES)
    parts = [w_in[:, offs[i]:offs[i + 1]] for i in range(len(IN_SIZES))]
    q_m, k_m, v_m, o_m, i_g, f_g, c_q, c_kv, k_r = parts
    gk = _rope_block_cols(k_r).at[:, 0:2 * MLSTM_HEADS].set(jnp.concatenate([i_g, f_g], axis=1))
    return jnp.concatenate([q_m, k_m, v_m, o_m, c_q, c_kv, gk], axis=1).astype(BF16)


def _prep_w_uq(w_uq):
    w = w_uq.reshape(Q_LORA, MLA_HEADS, HEAD_DIM + ROPE_DIM)
    blk = jnp.concatenate([w[..., :HEAD_DIM], _rope_block_cols(w[..., HEAD_DIM:])], axis=-1)
    return blk.reshape(Q_LORA, MLA_HEADS * QK_HEAD).astype(BF16)


def _prep_w_ukv(w_ukv):
    w = w_ukv.reshape(KV_LORA, MLA_HEADS, 2 * HEAD_DIM)
    k_nope = w[..., :HEAD_DIM].reshape(KV_LORA, MLA_WIDTH)
    v = w[..., HEAD_DIM:].reshape(KV_LORA, MLA_WIDTH)
    return jnp.concatenate([k_nope, v], axis=1).astype(BF16)


def _rope_tables():
    inv_freq = ROPE_THETA ** (-jnp.arange(0, ROPE_DIM, 2, dtype=F32) / ROPE_DIM)
    n_blk = ROW_TILE // LANES
    freq = jnp.tile(inv_freq, n_blk)[None, :]
    selc = np.zeros((n_blk, 2 * LANES, LANES), np.float32)
    sels = np.zeros((n_blk, 2 * LANES, LANES), np.float32)
    f = np.arange(ROPE_HALF)
    for k in range(n_blk):
        for half in range(2):
            rows = half * LANES + ROPE_HALF * k + f
            selc[k, rows, ROPE_LO + f] = 1.0
            selc[k, rows, ROPE_HI + f] = 1.0
            sels[k, rows, ROPE_LO + f] = -1.0
            sels[k, rows, ROPE_HI + f] = 1.0
    return freq, jnp.asarray(selc, BF16), jnp.asarray(sels, BF16)


def _pack_positions(positions):
    n_blk = ROW_TILE // LANES
    pos = positions.astype(F32).reshape(-1, n_blk, LANES).transpose(0, 2, 1)
    return jnp.repeat(pos, LANES // n_blk, axis=-1).reshape(-1, LANES)


def kernel(x, positions, even_w_in, even_b_igate, even_b_fgate, even_mlstm_norm, even_q_norm,
           even_kv_norm, even_w_uq, even_w_ukv, even_w_out, odd_pool_w, odd_layer_scale,
           ffn_w_up, ffn_conv_w, ffn_conv_b, ffn_w_down, ln_mix_g, ln_mix_b, ln_ffn_g, ln_ffn_b):
    bsz, s, d = x.shape
    t = bsz * s
    row = lambda v: v.reshape(1, -1)

    freq, selc, sels = _rope_tables()
    qm, km, vm, om, gates, qa, ka, va = _inproj(
        x.reshape(t, d), _pack_positions(positions), _prep_w_in(even_w_in[0]),
        row(even_q_norm[0]), row(even_kv_norm[0]), _prep_w_uq(even_w_uq[0]), _prep_w_ukv(even_w_ukv[0]),
        freq, selc, sels)

    gate_bias = jnp.concatenate([even_b_igate[0], even_b_fgate[0]])
    b3 = lambda a: a.reshape(bsz, s, a.shape[-1])
    ym = _mlstm(b3(qm), b3(km), b3(vm), b3(om), gates, gate_bias.reshape(-1, 1),
                row(even_mlstm_norm[0]), _mlstm_selector(), _mlstm_tri())
    ya = _attention(b3(qa), b3(ka), b3(va))

    stack_rows = lambda v: v.reshape(DEPTH, 1, -1)
    gate_half = jnp.concatenate([jnp.full((FFN_DIM,), 0.5, F32), jnp.ones((FFN_DIM,), F32)])
    ffn_params = (ffn_w_up.astype(BF16), ffn_conv_w * gate_half, stack_rows(ffn_conv_b * gate_half),
                  ffn_w_down.astype(BF16), stack_rows(ln_ffn_g), stack_rows(ln_ffn_b))
    x1 = _outproj_ffn(ym, ya, x, even_w_out[0].astype(BF16), row(ln_mix_g[0]), row(ln_mix_b[0]), 0, ffn_params)

    x2 = _pool(x1, odd_pool_w[0].astype(BF16), row(odd_layer_scale[0]), row(ln_mix_g[1]), row(ln_mix_b[1]))
    return _ffn(x2, 1, ffn_params)
```

```python
import jax
import jax.numpy as jnp
import numpy as np
from jax import lax
from jax.experimental import pallas as pl
from jax.experimental.pallas import tpu as pltpu

F32 = jnp.float32
BF16 = jnp.bfloat16

D_MODEL = 1024
DEPTH = 2
MLSTM_HEADS = 4
HEAD_DIM = 128
MLSTM_WIDTH = MLSTM_HEADS * HEAD_DIM
MLA_HEADS = 4
ROPE_DIM = 64
ROPE_HALF = ROPE_DIM // 2
Q_LORA = 256
KV_LORA = 128
MLA_WIDTH = MLA_HEADS * HEAD_DIM
ROPE_THETA = 10000.0
POOL_WINDOWS = (2, 4, 8, 16)
POOL_GROUP_DIM = D_MODEL // len(POOL_WINDOWS)
FFN_DIM = 2816
CONV_WIDTH = 3
LN_EPS = 1e-5
RMS_EPS = 1e-6
DEEPNORM_ALPHA = (2 * DEPTH) ** 0.25
IN_SIZES = (MLSTM_WIDTH, MLSTM_WIDTH, MLSTM_WIDTH, MLSTM_WIDTH, MLSTM_HEADS, MLSTM_HEADS,
            Q_LORA, KV_LORA, ROPE_DIM)

LANES = 128
SUBLANES = 8
VMEM_LIMIT_BYTES = 56 * 1024 * 1024

ROPE_LO = 8
ROPE_HI = ROPE_LO + LANES // 2
QK_HEAD = 2 * LANES

ROW_TILE = 512
MLSTM_CHUNK = 256
MLSTM_BLOCK = 2048
MLSTM_GROUP = 4
ATTN_TILE = 512
FFN_COLS = 256
FFN_ROW_TILE = 1024
FFN_SUB_ROWS = 128
POOL_ROW_TILE = 1024
ROW_BLOCKS = 4
POOL_HALO = 16


def _const_spec(shape):
    nd = len(shape)
    return pl.BlockSpec(shape, lambda *_: (0,) * nd, pipeline_mode=pl.Buffered(1))


def _layer_norm(z, g, b):
    mu = jnp.mean(z, axis=-1, keepdims=True)
    d = z - mu
    var = jnp.mean(d * d, axis=-1, keepdims=True)
    return d * lax.rsqrt(var + LN_EPS) * g + b


def _rms_norm(z, g):
    return z * lax.rsqrt(jnp.mean(z * z, axis=-1, keepdims=True) + RMS_EPS) * g


def _log_sigmoid(z):
    return -(jnp.maximum(-z, 0.0) + jnp.log1p(jnp.exp(-jnp.abs(z))))


def _split3(z):
    hi = z.astype(BF16)
    r1 = z - hi.astype(F32)
    mid = r1.astype(BF16)
    lo = (r1 - mid.astype(F32)).astype(BF16)
    return hi, mid, lo


def _split2_lanes(z):
    hi = z.astype(BF16)
    return jnp.concatenate([hi, (z - hi.astype(F32)).astype(BF16)], axis=1)


def _inproj_kernel(x_ref, pos_ref, w_in_ref, qn_ref, kvn_ref, wuq_ref, wukv_ref, freq_ref, selc_ref,
                   sels_ref, qm_ref, km_ref, vm_ref, om_ref, gates_ref, qa_ref, ka_ref, va_ref):
    xb = x_ref[...].astype(BF16)

    def proj(lo, width):
        return jnp.dot(xb, w_in_ref[:, lo:lo + width], preferred_element_type=F32)

    w = MLSTM_WIDTH
    c_q = proj(4 * w, Q_LORA)
    ckv_gk = proj(4 * w + Q_LORA, KV_LORA + LANES)
    c_kv = ckv_gk[:, :KV_LORA]
    gk = ckv_gk[:, KV_LORA:]
    gates_ref[...] = gk.T[0:2 * MLSTM_HEADS, :]

    ang = pos_ref[...] * freq_ref[...]
    cos2 = _split2_lanes(jnp.cos(ang))
    sin2 = _split2_lanes(jnp.sin(ang))
    n_blk = ROW_TILE // LANES
    cosm = jnp.concatenate([jnp.dot(cos2, selc_ref[k], preferred_element_type=F32) for k in range(n_blk)], axis=0)
    sinm = jnp.concatenate([jnp.dot(sin2, sels_ref[k], preferred_element_type=F32) for k in range(n_blk)], axis=0)

    def rope(blk):
        return blk * cosm + pltpu.roll(blk, LANES // 2, 1) * sinm

    k_rope = rope(gk).astype(BF16)

    qm_ref[...] = proj(0, w).astype(BF16)
    km_ref[...] = (proj(w, w) * (HEAD_DIM ** -0.5)).astype(BF16)
    vm_ref[...] = proj(2 * w, w).astype(BF16)
    om_ref[...] = proj(3 * w, w)

    scale = (HEAD_DIM + ROPE_DIM) ** -0.5 * float(np.log2(np.e))
    q = jnp.dot(_rms_norm(c_q, qn_ref[...]).astype(BF16), wuq_ref[...], preferred_element_type=F32)
    kv = jnp.dot(_rms_norm(c_kv, kvn_ref[...]).astype(BF16), wukv_ref[...], preferred_element_type=F32)
    for h in range(MLA_HEADS):
        lo = h * QK_HEAD
        qa_ref[:, lo:lo + LANES] = (q[:, lo:lo + LANES] * scale).astype(BF16)
        qa_ref[:, lo + LANES:lo + QK_HEAD] = (rope(q[:, lo + LANES:lo + QK_HEAD]) * scale).astype(BF16)
        ka_ref[:, lo:lo + LANES] = kv[:, h * HEAD_DIM:(h + 1) * HEAD_DIM].astype(BF16)
        ka_ref[:, lo + LANES:lo + QK_HEAD] = k_rope
    va_ref[...] = kv[:, MLA_WIDTH:].astype(BF16)


def _inproj(x2, pos_packed, w_in_p, q_norm, kv_norm, wuq_p, wukv_p, freq, selc, sels):
    t = x2.shape[0]
    tm = ROW_TILE
    row = lambda width: pl.BlockSpec((tm, width), lambda i: (i, 0))
    out_shapes = (
        jax.ShapeDtypeStruct((t, MLSTM_WIDTH), BF16),
        jax.ShapeDtypeStruct((t, MLSTM_WIDTH), BF16),
        jax.ShapeDtypeStruct((t, MLSTM_WIDTH), BF16),
        jax.ShapeDtypeStruct((t, MLSTM_WIDTH), F32),
        jax.ShapeDtypeStruct((2 * MLSTM_HEADS, t), F32),
        jax.ShapeDtypeStruct((t, MLA_HEADS * QK_HEAD), BF16),
        jax.ShapeDtypeStruct((t, MLA_HEADS * QK_HEAD), BF16),
        jax.ShapeDtypeStruct((t, MLA_WIDTH), BF16),
    )
    return pl.pallas_call(
        _inproj_kernel,
        out_shape=out_shapes,
        grid=(t // tm,),
        in_specs=[row(D_MODEL), pl.BlockSpec((tm // 4, LANES), lambda i: (i, 0)),
                  _const_spec(w_in_p.shape), _const_spec(q_norm.shape),
                  _const_spec(kv_norm.shape), _const_spec(wuq_p.shape), _const_spec(wukv_p.shape),
                  _const_spec(freq.shape), _const_spec(selc.shape), _const_spec(sels.shape)],
        out_specs=tuple(pl.BlockSpec((2 * MLSTM_HEADS, tm), lambda i: (0, i)) if s.shape[0] != t else row(s.shape[1])
                        for s in out_shapes),
        compiler_params=pltpu.CompilerParams(dimension_semantics=("parallel",),
                                             vmem_limit_bytes=VMEM_LIMIT_BYTES),
        name="inproj",
    )(x2, pos_packed, w_in_p, q_norm, kv_norm, wuq_p, wukv_p, freq, selc, sels)


def _mlstm_kernel(q_ref, k_ref, v_ref, o_ref, gt_ref, brow_ref, nw_ref, sel_ref, tri_ref, y_ref, ct_ref, m_ref,
                  a8_ref, xt_ref):
    L = MLSTM_CHUNK
    d = HEAD_DIM
    nh = MLSTM_HEADS
    heads = range(nh)

    @pl.when(pl.program_id(1) == 0)
    def _():
        ct_ref[...] = jnp.zeros_like(ct_ref)
        m_ref[...] = jnp.zeros_like(m_ref)

    causal = lax.broadcasted_iota(jnp.int32, (L, L), 0) >= lax.broadcasted_iota(jnp.int32, (L, L), 1)
    top_rows = lax.broadcasted_iota(jnp.int32, (2 * nh, L), 0) < nh
    ones_blk = jnp.ones((L, d), BF16)
    neg_inf = jnp.float32(-jnp.inf)

    def hs(h):
        return slice(h * d, (h + 1) * d)

    def lanes(j):
        return slice(j * LANES, (j + 1) * LANES)

    def scalar_part(c, slot):
        g8 = gt_ref[:, pl.ds(pl.multiple_of(c * L, L), L)] + brow_ref[...]
        b8 = sum(jnp.dot(p, tri_ref[...], preferred_element_type=F32) for p in _split3(_log_sigmoid(g8)))
        a8 = g8 - pltpu.roll(b8, nh, 0)
        w8 = jnp.exp(a8 - jnp.max(a8, axis=-1, keepdims=True))
        r = jnp.where(top_rows, w8, b8)
        rpad = jnp.concatenate([p.astype(F32) for p in _split3(r)]
                               + [jnp.zeros((LANES - 3 * 2 * nh, L), F32)], axis=0)
        a8_ref[slot] = a8
        xt_ref[slot] = rpad.T.astype(BF16)

    def local_part(c, slot):
        rows = pl.ds(pl.multiple_of(c * L, L), L)
        a8 = a8_ref[slot]
        bc = jnp.dot(xt_ref[slot], sel_ref[...], preferred_element_type=F32)
        q = [q_ref[0, rows, hs(h)] for h in heads]
        k = [k_ref[0, rows, hs(h)] for h in heads]
        s = [lax.dot_general(q[h], k[h], (((1,), (1,)), ((), ())), preferred_element_type=F32) for h in heads]
        a_low = [jnp.where(causal, a8[h:h + 1, :], neg_inf) for h in heads]
        mp = [jnp.max(a_low[h], axis=-1, keepdims=True) for h in heads]
        a_mat = [(s[h] * jnp.exp(a_low[h] - mp[h])).astype(BF16) for h in heads]
        v_aug = [jnp.concatenate([v_ref[0, rows, hs(h)], ones_blk], axis=1) for h in heads]
        intra = [jnp.dot(a_mat[h], v_aug[h], preferred_element_type=F32) for h in heads]
        w_rep = [bc[:, lanes(h)] for h in heads]
        b_rep = [bc[:, lanes(nh + h)] for h in heads]
        wv = [jnp.concatenate([(v_aug[h][:, :d].astype(F32) * w_rep[h]).astype(BF16), w_rep[h].astype(BF16)], axis=1)
              for h in heads]
        upd = [lax.dot_general(k[h], wv[h], (((0,), (0,)), ((), ())), preferred_element_type=F32) for h in heads]
        return rows, q, mp, b_rep, intra, upd

    def carried_part(local):
        rows, q, mp, b_rep, intra, upd = local
        ct = [ct_ref[h] for h in heads]
        inter = [jnp.dot(q[h], ct[h].astype(BF16), preferred_element_type=F32) for h in heads]
        m_prev = [m_ref[h][0:1, :] for h in heads]
        for h in heads:
            mp_rep = jnp.broadcast_to(mp[h], (L, LANES))
            m_rep = jnp.maximum(mp_rep, m_prev[h])
            e_intra = jnp.exp(mp_rep - m_rep)
            e_inter = jnp.exp(m_prev[h] - m_rep)
            num = e_intra * intra[h][:, :d] + e_inter * inter[h][:, :d]
            den = e_intra * intra[h][:, d:] + e_inter * inter[h][:, d:]
            hraw = num / jnp.maximum(jnp.abs(den), jnp.exp(-(b_rep[h] + m_rep)))
            gate = jax.nn.sigmoid(o_ref[0, rows, hs(h)])
            y_ref[0, rows, hs(h)] = (_rms_norm(hraw, nw_ref[:, hs(h)]) * gate).astype(BF16)
            mp_last = mp_rep[L - 1:L, :]
            m_last = jnp.maximum(mp_last, m_prev[h])
            keep = jnp.exp(m_prev[h] - m_last)
            add = jnp.exp(mp_last - m_last)
            ct_ref[h] = (jnp.concatenate([keep, keep], axis=1) * ct[h]
                         + jnp.concatenate([add, add], axis=1) * upd[h])
            m_ref[h] = jnp.broadcast_to(b_rep[h][L - 1:L, :] + m_last, (SUBLANES, LANES))

    group = MLSTM_GROUP
    n_groups = MLSTM_BLOCK // (group * L)
    for g in range(group):
        scalar_part(g, g)

    def step(i, carry):
        local = [local_part(group * i + g, g) for g in range(group)]
        for g in range(group):
            carried_part(local[g])
        nxt = jnp.minimum(i + 1, n_groups - 1)
        for g in range(group):
            scalar_part(group * nxt + g, g)
        return carry

    lax.fori_loop(0, n_groups, step, 0)


def _mlstm(qm, km, vm, om, gates_t, b_row, norm_w, sel, tri_t):
    b, s, _ = qm.shape
    tb = MLSTM_BLOCK
    blk = lambda width: pl.BlockSpec((1, tb, width), lambda bi, si: (bi, si, 0))
    return pl.pallas_call(
        _mlstm_kernel,
        out_shape=jax.ShapeDtypeStruct((b, s, MLSTM_WIDTH), BF16),
        grid=(b, s // tb),
        in_specs=[blk(MLSTM_WIDTH), blk(MLSTM_WIDTH), blk(MLSTM_WIDTH), blk(MLSTM_WIDTH),
                  pl.BlockSpec((2 * MLSTM_HEADS, tb), lambda bi, si: (0, bi * (s // tb) + si)),
                  _const_spec(b_row.shape), _const_spec(norm_w.shape), _const_spec(sel.shape),
                  _const_spec(tri_t.shape)],
        out_specs=blk(MLSTM_WIDTH),
        scratch_shapes=[pltpu.VMEM((MLSTM_HEADS, HEAD_DIM, 2 * HEAD_DIM), F32),
                        pltpu.VMEM((MLSTM_HEADS, SUBLANES, LANES), F32),
                        pltpu.VMEM((MLSTM_GROUP, 2 * MLSTM_HEADS, MLSTM_CHUNK), F32),
                        pltpu.VMEM((MLSTM_GROUP, MLSTM_CHUNK, LANES), BF16)],
        compiler_params=pltpu.CompilerParams(dimension_semantics=("parallel", "arbitrary"),
                                             vmem_limit_bytes=VMEM_LIMIT_BYTES),
        name="mlstm",
    )(qm, km, vm, om, gates_t, b_row, norm_w, sel, tri_t)


def _mlstm_tri():
    idx = np.arange(MLSTM_CHUNK)
    return jnp.asarray(idx[:, None] <= idx[None, :], BF16)


def _mlstm_selector():
    r = np.arange(LANES)[:, None]
    c = np.arange(2 * MLSTM_HEADS * LANES)[None, :]
    return jnp.asarray((r < 3 * 2 * MLSTM_HEADS) & (r % (2 * MLSTM_HEADS) == c // LANES), BF16)


def _attn_kernel(q_ref, k_ref, v_ref, o_ref, s_ref, m_ref, acc_ref):
    t = ATTN_TILE
    nh = MLA_HEADS
    p = pl.program_id(1)
    half_tiles = pl.num_programs(1)
    diag = (lax.broadcasted_iota(jnp.int32, (t, t), 0) >= lax.broadcasted_iota(jnp.int32, (t, t), 1))
    neg_inf = jnp.float32(-jnp.inf)
    ones_blk = jnp.ones((t, HEAD_DIM), BF16)

    def reset():
        m_ref[...] = jnp.full(m_ref.shape, neg_inf, F32)
        acc_ref[...] = jnp.zeros_like(acc_ref)

    def scores(tile, j, slot, h):
        r0 = pl.multiple_of(j * t, t)
        q = q_ref[0, tile, :, h * QK_HEAD:(h + 1) * QK_HEAD]
        k = k_ref[0, pl.ds(r0, t), h * QK_HEAD:(h + 1) * QK_HEAD]
        s_ref[slot * nh + h] = lax.dot_general(q, k, (((1,), (1,)), ((), ())),
                                               preferred_element_type=F32)

    def accumulate(j, slot, h, masked):
        r0 = pl.multiple_of(j * t, t)
        v_aug = jnp.concatenate(
            [v_ref[0, pl.ds(r0, t), h * HEAD_DIM:(h + 1) * HEAD_DIM], ones_blk], axis=1)
        s = s_ref[slot * nh + h]
        if masked:
            s = jnp.where(diag, s, neg_inf)
        m_old = m_ref[h]
        m_new = jnp.maximum(m_old, jnp.max(s, axis=-1, keepdims=True))
        pr = jnp.exp2(s - jnp.concatenate([m_new] * (t // LANES), axis=1)).astype(BF16)
        alpha = jnp.exp2(m_old - m_new)
        acc_ref[h] = (jnp.concatenate([alpha, alpha], axis=1) * acc_ref[h]
                      + jnp.dot(pr, v_aug, preferred_element_type=F32))
        m_ref[h] = m_new

    def finish(tile, h):
        acc = acc_ref[h]
        o_ref[0, tile, :, h * HEAD_DIM:(h + 1) * HEAD_DIM] = (
            acc[:, :HEAD_DIM] / acc[:, HEAD_DIM:]).astype(BF16)

    def half_step(tile, j, slot_in, slot_out):
        for h in range(nh):
            scores(tile, j + 1, slot_out, h)
            accumulate(j, slot_in, h, masked=False)

    reset()
    for h in range(nh):
        scores(0, 0, 0, h)

    def body_a(i, carry):
        half_step(0, 2 * i, 0, 1)

        @pl.when(2 * i + 1 < p)
        def _():
            half_step(0, 2 * i + 1, 1, 0)
        return carry

    lax.fori_loop(0, (p + 1) // 2, body_a, 0)
    for parity in range(2):
        @pl.when(lax.rem(p, 2) == parity)
        def _(parity=parity):
            for h in range(nh):
                scores(1, 0, 2, h)
                accumulate(p, parity, h, masked=True)
                finish(0, h)

    reset()
    last = p + half_tiles
    half_step(1, 0, 2, 1)

    def body_b(i, carry):
        half_step(1, 2 * i + 1, 1, 0)

        @pl.when(2 * i + 2 < last)
        def _():
            half_step(1, 2 * i + 2, 0, 1)
        return carry

    lax.fori_loop(0, last // 2, body_b, 0)
    for h in range(nh):
        accumulate(last, lax.rem(last, 2), h, masked=True)
        finish(1, h)


def _attention(qa, ka, va):
    b, s, _ = qa.shape
    t = ATTN_TILE
    half = s // 2
    tile_pair = lambda width: pl.BlockSpec((1, 2, t, width), lambda bi, p: (bi, 0, p, 0))
    out = pl.pallas_call(
        _attn_kernel,
        out_shape=jax.ShapeDtypeStruct((b, 2, half, MLA_WIDTH), BF16),
        grid=(b, half // t),
        in_specs=[tile_pair(MLA_HEADS * QK_HEAD),
                  pl.BlockSpec((1, s, MLA_HEADS * QK_HEAD), lambda bi, p: (bi, 0, 0)),
                  pl.BlockSpec((1, s, MLA_WIDTH), lambda bi, p: (bi, 0, 0))],
        out_specs=tile_pair(MLA_WIDTH),
        scratch_shapes=[pltpu.VMEM((3 * MLA_HEADS, t, t), F32),
                        pltpu.VMEM((MLA_HEADS, t, LANES), F32),
                        pltpu.VMEM((MLA_HEADS, t, 2 * HEAD_DIM), F32)],
        compiler_params=pltpu.CompilerParams(dimension_semantics=("parallel", "arbitrary"),
                                             vmem_limit_bytes=VMEM_LIMIT_BYTES),
        name="mla_attention",
    )(qa.reshape(b, 2, half, MLA_HEADS * QK_HEAD), ka, va)
    return out.reshape(b, s, MLA_WIDTH)


def _shift_rows(u, tail, k):
    return pltpu.roll(jnp.concatenate([tail, u], axis=0), k, 0)[SUBLANES:]


def _mlp_tile(res_ref, xb_ref, wup_ref, cw_ref, cb_ref, wdn_ref, g_ref, b_ref, o_ref, tail_ref, h_ref, n_split):
    tm = FFN_ROW_TILE
    tf = FFN_COLS
    sub = FFN_SUB_ROWS
    assert CONV_WIDTH == 3 and CONV_WIDTH - 1 <= SUBLANES

    def conv_cols(lo, r0, tail):
        u = jnp.dot(xb_ref[r0:r0 + sub, :], wup_ref[:, lo:lo + tf], preferred_element_type=F32)
        uc = (_shift_rows(u, tail, 2) * cw_ref[0:1, lo:lo + tf]
              + _shift_rows(u, tail, 1) * cw_ref[1:2, lo:lo + tf]
              + u * cw_ref[2:3, lo:lo + tf] + cb_ref[:, lo:lo + tf])
        return uc, u[sub - SUBLANES:, :]

    for c in range(FFN_DIM // tf):
        glo, vlo = c * tf, FFN_DIM + c * tf
        gtail = tail_ref[:, glo:glo + tf]
        vtail = tail_ref[:, vlo:vlo + tf]
        for r0 in range(0, tm, sub):
            gate, gtail = conv_cols(glo, r0, gtail)
            val, vtail = conv_cols(vlo, r0, vtail)
            h_ref[r0:r0 + sub, c * tf:(c + 1) * tf] = ((gate + gate * jnp.tanh(gate)) * val).astype(BF16)
        tail_ref[:, glo:glo + tf] = gtail
        tail_ref[:, vlo:vlo + tf] = vtail

    blk = tm // n_split
    ys = [jnp.dot(h_ref[r * blk:(r + 1) * blk, :], wdn_ref[...], preferred_element_type=F32)
          for r in range(n_split)]
    for r in range(n_split):
        rs = slice(r * blk, (r + 1) * blk)
        o_ref[0, rs, :] = _layer_norm(DEEPNORM_ALPHA * res_ref[0, rs, :] + ys[r], g_ref[...], b_ref[...])


def _ffn_kernel(x_ref, wup_ref, cw_ref, cb_ref, wdn_ref, g_ref, b_ref, o_ref, tail_ref, h_ref, xb_ref):
    @pl.when(pl.program_id(1) == 0)
    def _():
        tail_ref[...] = jnp.zeros_like(tail_ref)

    xb_ref[...] = x_ref[0].astype(BF16)
    _mlp_tile(x_ref, xb_ref, wup_ref, cw_ref, cb_ref, wdn_ref, g_ref, b_ref, o_ref, tail_ref, h_ref,
              n_split=ROW_BLOCKS)


def _outproj_ffn_kernel(ym_ref, ya_ref, x_ref, wout_ref, gmix_ref, bmix_ref, wup_ref, cw_ref, cb_ref, wdn_ref,
                        g_ref, b_ref, o_ref, tail_ref, h_ref, xb_ref):
    tm = FFN_ROW_TILE

    @pl.when(pl.program_id(1) == 0)
    def _():
        tail_ref[...] = jnp.zeros_like(tail_ref)

    blk = tm // ROW_BLOCKS
    ys = []
    for r in range(ROW_BLOCKS):
        rs = slice(r * blk, (r + 1) * blk)
        ys.append(jnp.dot(ym_ref[0, rs, :], wout_ref[0:MLSTM_WIDTH, :], preferred_element_type=F32)
                  + jnp.dot(ya_ref[0, rs, :], wout_ref[MLSTM_WIDTH:, :], preferred_element_type=F32))
    for r in range(ROW_BLOCKS):
        rs = slice(r * blk, (r + 1) * blk)
        x1 = _layer_norm(DEEPNORM_ALPHA * x_ref[0, rs, :] + ys[r], gmix_ref[...], bmix_ref[...])
        o_ref[0, rs, :] = x1
        xb_ref[rs, :] = x1.astype(BF16)
    _mlp_tile(o_ref, xb_ref, wup_ref, cw_ref, cb_ref, wdn_ref, g_ref, b_ref, o_ref, tail_ref, h_ref, n_split=1)


def _layer_spec(shape, layer):
    nd = len(shape) - 1
    return pl.BlockSpec((None,) + tuple(shape[1:]), lambda *_: (layer,) + (0,) * nd,
                        pipeline_mode=pl.Buffered(1))


def _ffn_call(body, name, bsz, s, row_inputs, const_inputs, layer, stacked):
    tm = FFN_ROW_TILE
    blk = lambda width: pl.BlockSpec((1, tm, width), lambda bi, si: (bi, si, 0))
    return pl.pallas_call(
        body,
        out_shape=jax.ShapeDtypeStruct((bsz, s, D_MODEL), F32),
        grid=(bsz, s // tm),
        in_specs=([blk(a.shape[-1]) for a in row_inputs] + [_const_spec(a.shape) for a in const_inputs]
                  + [_layer_spec(a.shape, layer) for a in stacked]),
        out_specs=blk(D_MODEL),
        scratch_shapes=[pltpu.VMEM((SUBLANES, 2 * FFN_DIM), F32),
                        pltpu.VMEM((tm, FFN_DIM), BF16),
                        pltpu.VMEM((tm, D_MODEL), BF16)],
        compiler_params=pltpu.CompilerParams(dimension_semantics=("parallel", "arbitrary"),
                                             vmem_limit_bytes=VMEM_LIMIT_BYTES),
        name=name,
    )(*row_inputs, *const_inputs, *stacked)


def _ffn(x3, layer, stacked):
    bsz, s, _ = x3.shape
    return _ffn_call(_ffn_kernel, "conv_ffn_ln", bsz, s, (x3,), (), layer, stacked)


def _outproj_ffn(ym3, ya3, x3, w_out, g_mix, b_mix, layer, stacked):
    bsz, s, _ = x3.shape
    return _ffn_call(_outproj_ffn_kernel, "outproj_ffn_ln", bsz, s, (ym3, ya3, x3), (w_out, g_mix, b_mix),
                     layer, stacked)


def _pool_kernel(x_ref, pw_ref, ls_ref, g_ref, b_ref, o_ref, xbuf_ref):
    tm = POOL_ROW_TILE
    si = pl.program_id(1)

    @pl.when(si == 0)
    def _():
        xbuf_ref[0:POOL_HALO, :] = jnp.zeros((POOL_HALO, D_MODEL), F32)

    xbuf_ref[POOL_HALO:, :] = x_ref[0]
    sub = tm // ROW_BLOCKS
    ys = []
    for r in range(ROW_BLOCKS):
        r0 = r * sub
        t_pos = si * tm + r0 + lax.broadcasted_iota(jnp.int32, (sub, 1), 0)
        parts = []
        for gi, w in enumerate(POOL_WINDOWS):
            cols = slice(gi * POOL_GROUP_DIM, (gi + 1) * POOL_GROUP_DIM)
            ext = xbuf_ref[r0:r0 + POOL_HALO + sub, cols]
            k = 1
            while k < w:
                ext = ext + pltpu.roll(ext, k, 0)
                k *= 2
            cur = xbuf_ref[POOL_HALO + r0:POOL_HALO + r0 + sub, cols]
            cnt = jnp.minimum(t_pos + 1, w).astype(F32)
            pooled = (ext[POOL_HALO:, :] / cnt - cur).astype(BF16)
            parts.append(jnp.dot(pooled, pw_ref[gi], preferred_element_type=F32))
        ys.append(jnp.concatenate(parts, axis=1) * ls_ref[...])
    for r in range(ROW_BLOCKS):
        rs = slice(r * sub, (r + 1) * sub)
        o_ref[0, rs, :] = _layer_norm(DEEPNORM_ALPHA * x_ref[0, rs, :] + ys[r], g_ref[...], b_ref[...])
    xbuf_ref[0:POOL_HALO, :] = x_ref[0, tm - POOL_HALO:, :]


def _pool(x3, pool_w, layer_scale, g, b):
    bsz, s, _ = x3.shape
    tm = POOL_ROW_TILE
    blk = pl.BlockSpec((1, tm, D_MODEL), lambda bi, si: (bi, si, 0))
    return pl.pallas_call(
        _pool_kernel,
        out_shape=jax.ShapeDtypeStruct(x3.shape, F32),
        grid=(bsz, s // tm),
        in_specs=[blk, _const_spec(pool_w.shape), _const_spec(layer_scale.shape),
                  _const_spec(g.shape), _const_spec(b.shape)],
        out_specs=blk,
        scratch_shapes=[pltpu.VMEM((POOL_HALO + tm, D_MODEL), F32)],
        compiler_params=pltpu.CompilerParams(dimension_semantics=("parallel", "arbitrary"),
                                             vmem_limit_bytes=VMEM_LIMIT_BYTES),
        name="pool_ln",
    )(x3, pool_w, layer_scale, g, b)


def _rope_block_cols(w_rope):
    z = lambda n: jnp.zeros(w_rope.shape[:-1] + (n,), w_rope.dtype)
    return jnp.concatenate([z(ROPE_LO), w_rope[..., :ROPE_HALF], z(ROPE_HI - ROPE_LO - ROPE_HALF),
                            w_rope[..., ROPE_HALF:], z(LANES - ROPE_HI - ROPE_HALF)], axis=-1)


def _prep_w_in(w_in):
    offs = np.cumsum((0,) + IN_SIZES)
    parts = [w_in[:, offs[i]:offs[i + 1]] for i in range(len(IN_SIZES))]
    q_m, k_m, v_m, o_m, i_g, f_g, c_q, c_kv, k_r = parts
    gk = _rope_block_cols(k_r).at[:, 0:2 * MLSTM_HEADS].set(jnp.concatenate([i_g, f_g], axis=1))
    return jnp.concatenate([q_m, k_m, v_m, o_m, c_q, c_kv, gk], axis=1).astype(BF16)


def _prep_w_uq(w_uq):
    w = w_uq.reshape(Q_LORA, MLA_HEADS, HEAD_DIM + ROPE_DIM)
    blk = jnp.concatenate([w[..., :HEAD_DIM], _rope_block_cols(w[..., HEAD_DIM:])], axis=-1)
    return blk.reshape(Q_LORA, MLA_HEADS * QK_HEAD).astype(BF16)


def _prep_w_ukv(w_ukv):
    w = w_ukv.reshape(KV_LORA, MLA_HEADS, 2 * HEAD_DIM)
    k_nope = w[..., :HEAD_DIM].reshape(KV_LORA, MLA_WIDTH)
    v = w[..., HEAD_DIM:].reshape(KV_LORA, MLA_WIDTH)
    return jnp.concatenate([k_nope, v], axis=1).astype(BF16)


def _rope_tables():
    inv_freq = ROPE_THETA ** (-jnp.arange(0, ROPE_DIM, 2, dtype=F32) / ROPE_DIM)
    n_blk = ROW_TILE // LANES
    freq = jnp.tile(inv_freq, n_blk)[None, :]
    selc = np.zeros((n_blk, 2 * LANES, LANES), np.float32)
    sels = np.zeros((n_blk, 2 * LANES, LANES), np.float32)
    f = np.arange(ROPE_HALF)
    for k in range(n_blk):
        for half in range(2):
            rows = half * LANES + ROPE_HALF * k + f
            selc[k, rows, ROPE_LO + f] = 1.0
            selc[k, rows, ROPE_HI + f] = 1.0
            sels[k, rows, ROPE_LO + f] = -1.0
            sels[k, rows, ROPE_HI + f] = 1.0
    return freq, jnp.asarray(selc, BF16), jnp.asarray(sels, BF16)


def _pack_positions(positions):
    n_blk = ROW_TILE // LANES
    pos = positions.astype(F32).reshape(-1, n_blk, LANES).transpose(0, 2, 1)
    return jnp.repeat(pos, LANES // n_blk, axis=-1).reshape(-1, LANES)


def kernel(x, positions, even_w_in, even_b_igate, even_b_fgate, even_mlstm_norm, even_q_norm,
           even_kv_norm, even_w_uq, even_w_ukv, even_w_out, odd_pool_w, odd_layer_scale,
           ffn_w_up, ffn_conv_w, ffn_conv_b, ffn_w_down, ln_mix_g, ln_mix_b, ln_ffn_g, ln_ffn_b):
    bsz, s, d = x.shape
    t = bsz * s
    row = lambda v: v.reshape(1, -1)

    freq, selc, sels = _rope_tables()
    qm, km, vm, om, gates, qa, ka, va = _inproj(
        x.reshape(t, d), _pack_positions(positions), _prep_w_in(even_w_in[0]),
        row(even_q_norm[0]), row(even_kv_norm[0]), _prep_w_uq(even_w_uq[0]), _prep_w_ukv(even_w_ukv[0]),
        freq, selc, sels)

    gate_bias = jnp.concatenate([even_b_igate[0], even_b_fgate[0]])
    b3 = lambda a: a.reshape(bsz, s, a.shape[-1])
    ym = _mlstm(b3(qm), b3(km), b3(vm), b3(om), gates, gate_bias.reshape(-1, 1),
                row(even_mlstm_norm[0]), _mlstm_selector(), _mlstm_tri())
    ya = _attention(b3(qa), b3(ka), b3(va))

    stack_rows = lambda v: v.reshape(DEPTH, 1, -1)
    gate_half = jnp.concatenate([jnp.full((FFN_DIM,), 0.5, F32), jnp.ones((FFN_DIM,), F32)])
    ffn_params = (ffn_w_up.astype(BF16), ffn_conv_w * gate_half, stack_rows(ffn_conv_b * gate_half),
                  ffn_w_down.astype(BF16), stack_rows(ln_ffn_g), stack_rows(ln_ffn_b))
    x1 = _outproj_ffn(ym, ya, x, even_w_out[0].astype(BF16), row(ln_mix_g[0]), row(ln_mix_b[0]), 0, ffn_params)

    x2 = _pool(x1, odd_pool_w[0].astype(BF16), row(odd_layer_scale[0]), row(ln_mix_g[1]), row(ln_mix_b[1]))
    return _ffn(x2, 1, ffn_params)
```
